```python
import math
import jax, jax.numpy as jnp
from jax import lax
import numpy as np

D_MODEL = 4096
BATCH = 4
SEQ = 4096
DEPTH = 1

GRID_W = 64
ROPE_THETA = 10000.0
NORM_EPS = 1e-6

MIX_WIDTH = D_MODEL
ATTN_WIDTH = MIX_WIDTH // 2
RET_WIDTH = MIX_WIDTH - ATTN_WIDTH

ATTN_HEAD_DIM = 128
ATTN_HEADS = ATTN_WIDTH // ATTN_HEAD_DIM
ATTN_KV_HEADS = 4
ATTN_GROUP = ATTN_HEADS // ATTN_KV_HEADS
Q_BLOCK = 128

RET_HEAD_DIM = 256
RET_HEADS = RET_WIDTH // RET_HEAD_DIM
RET_CHUNK = 128

ATTN_KV_WIDTH = ATTN_KV_HEADS * ATTN_HEAD_DIM
SPLIT_SIZES = (ATTN_WIDTH, ATTN_KV_WIDTH, ATTN_KV_WIDTH, RET_WIDTH, RET_WIDTH, RET_WIDTH, RET_WIDTH)
IN_PROJ_DIM = sum(SPLIT_SIZES)

N_EXPERTS = 16
EXPERT_FF = D_MODEL // 2
CAPACITY_FACTOR = 2

kernel_name = "hybrid_gqa_retention_ec_moe_encoder"


def rmsnorm(x, g):
    xf = x.astype(jnp.float32)
    y = xf * lax.rsqrt(jnp.mean(xf * xf, axis=-1, keepdims=True) + NORM_EPS)
    return (y * g.astype(jnp.float32)).astype(x.dtype)


def modulate(x, g, shift, scale):
    return rmsnorm(x, g) * (1.0 + scale[:, None, :]) + shift[:, None, :]


def axial_rope(x, row, col):
    d = x.shape[-1]
    quarter = d // 4
    inv = ROPE_THETA ** (-jnp.arange(quarter, dtype=jnp.float32) / quarter)
    ang = jnp.concatenate([row[:, None] * inv, col[:, None] * inv], axis=-1)
    cos = jnp.cos(ang)[None, :, None, :]
    sin = jnp.sin(ang)[None, :, None, :]
    xf = x.astype(jnp.float32).reshape(x.shape[:-1] + (d // 2, 2))
    x1, x2 = xf[..., 0], xf[..., 1]
    out = jnp.stack([x1 * cos - x2 * sin, x1 * sin + x2 * cos], axis=-1)
    return out.reshape(x.shape).astype(x.dtype)


def block_attention(q, k, v):
    B, KVH, G, S, d = q.shape
    nb = S // Q_BLOCK
    qb = jnp.moveaxis(q.reshape(B, KVH, G, nb, Q_BLOCK, d), 3, 0)
    scale = 1.0 / math.sqrt(d)

    def one_block(qblk):
        s = jnp.einsum('bkgqd,bksd->bkgqs', qblk, k).astype(jnp.float32) * scale
        p = jax.nn.softmax(s, axis=-1).astype(v.dtype)
        return jnp.einsum('bkgqs,bksd->bkgqd', p, v)

    o = lax.map(one_block, qb)
    o = jnp.moveaxis(o, 0, 3).reshape(B, KVH, G, S, d)
    return o.transpose(0, 3, 1, 2, 4).reshape(B, S, KVH * G * d)


def retention_scan(q, k, v, log_gamma, strict):
    B, H, S, dk = q.shape
    dv = v.shape[-1]
    n = S // RET_CHUNK
    qc = jnp.moveaxis(q.reshape(B, H, n, RET_CHUNK, dk), 2, 0)
    kc = jnp.moveaxis(k.reshape(B, H, n, RET_CHUNK, dk), 2, 0)
    vc = jnp.moveaxis(v.reshape(B, H, n, RET_CHUNK, dv), 2, 0)
    idx = jnp.arange(RET_CHUNK, dtype=jnp.float32)
    diff = idx[:, None] - idx[None, :]
    mask = (diff > 0) if strict else (diff >= 0)
    decay_in = jnp.where(mask[None], jnp.exp(log_gamma[:, None, None] * jnp.maximum(diff, 0.0)[None]), 0.0)
    xi = jnp.exp(log_gamma[:, None] * (idx + 1.0))[None, :, :, None]
    zeta = jnp.exp(log_gamma[:, None] * (RET_CHUNK - 1.0 - idx))[None, :, :, None]
    chunk_decay = jnp.exp(log_gamma * RET_CHUNK)[None, :, None, None]

    def step(R, inp):
        qi, ki, vi = inp
        s = jnp.einsum('bhnd,bhmd->bhnm', qi, ki) * decay_in[None]
        inner = jnp.einsum('bhnm,bhmv->bhnv', s, vi)
        cross = jnp.einsum('bhnd,bhdv->bhnv', qi, R) * xi
        R = R * chunk_decay + jnp.einsum('bhmd,bhmv->bhdv', ki * zeta, vi)
        return R, inner + cross

    R0 = jnp.zeros((B, H, dk, dv), jnp.float32)
    _, out = lax.scan(step, R0, (qc, kc, vc))
    return jnp.moveaxis(out, 0, 2).reshape(B, H, S, dv)


def setup_inputs(seed: int = 0) -> dict:
    key = jax.random.key(seed)
    ks = jax.random.split(key, 20)
    nrm = jax.random.normal
    f32 = jnp.float32
    base_decay = np.log(-np.log(1.0 - 2.0 ** (-5.0 - np.arange(RET_HEADS)))).astype(np.float32)
    return {
        "x": nrm(ks[0], (BATCH, SEQ, D_MODEL), f32),
        "c": nrm(ks[1], (BATCH, D_MODEL), f32),
        "w_ada": nrm(ks[2], (D_MODEL, 6 * D_MODEL), f32) * (0.5 * D_MODEL ** -0.5),
        "b_ada": 0.01 * nrm(ks[3], (6 * D_MODEL,), f32),
        "norm1_g": 1.0 + 0.01 * nrm(ks[4], (D_MODEL,), f32),
        "w_in": nrm(ks[5], (D_MODEL, IN_PROJ_DIM), f32) * D_MODEL ** -0.5,
        "attn_q_norm_g": 1.0 + 0.01 * nrm(ks[6], (ATTN_HEAD_DIM,), f32),
        "attn_k_norm_g": 1.0 + 0.01 * nrm(ks[7], (ATTN_HEAD_DIM,), f32),
        "ret_decay_fwd": jnp.asarray(base_decay) + 0.05 * nrm(ks[8], (RET_HEADS,), f32),
        "ret_decay_bwd": jnp.asarray(base_decay) + 0.05 * nrm(ks[9], (RET_HEADS,), f32),
        "ret_norm_g": 1.0 + 0.01 * nrm(ks[10], (RET_WIDTH,), f32),
        "w_out": nrm(ks[11], (MIX_WIDTH, D_MODEL), f32) * MIX_WIDTH ** -0.5,
        "norm2_g": 1.0 + 0.01 * nrm(ks[12], (D_MODEL,), f32),
        "w_router": nrm(ks[13], (D_MODEL, N_EXPERTS), f32) * D_MODEL ** -0.5,
        "w1": nrm(ks[14], (N_EXPERTS, D_MODEL, EXPERT_FF), f32) * D_MODEL ** -0.5,
        "w3": nrm(ks[15], (N_EXPERTS, D_MODEL, EXPERT_FF), f32) * D_MODEL ** -0.5,
        "w2": nrm(ks[16], (N_EXPERTS, EXPERT_FF, D_MODEL), f32) * EXPERT_FF ** -0.5,
        "final_g": 1.0 + 0.01 * nrm(ks[17], (D_MODEL,), f32),
    }


def reference(x, c, w_ada, b_ada, norm1_g, w_in, attn_q_norm_g, attn_k_norm_g,
              ret_decay_fwd, ret_decay_bwd, ret_norm_g, w_out, norm2_g, w_router,
              w1, w3, w2, final_g):
    B, S, D = x.shape
    rows = S // GRID_W
    row = jnp.repeat(jnp.arange(rows, dtype=jnp.float32), GRID_W)
    col = (jnp.arange(S) % GRID_W).astype(jnp.float32)

    mod = jax.nn.silu(c) @ w_ada + b_ada
    shift1, scale1, gate1, shift2, scale2, gate2 = jnp.split(mod, 6, axis=-1)

    for _ in range(DEPTH):
        h = modulate(x, norm1_g, shift1, scale1)
        proj = h @ w_in
        offs = np.cumsum(SPLIT_SIZES)[:-1].tolist()
        qa, ka, va, qr, kr, vr, gr = jnp.split(proj, offs, axis=-1)

        qa = axial_rope(rmsnorm(qa.reshape(B, S, ATTN_HEADS, ATTN_HEAD_DIM), attn_q_norm_g), row, col)
        ka = axial_rope(rmsnorm(ka.reshape(B, S, ATTN_KV_HEADS, ATTN_HEAD_DIM), attn_k_norm_g), row, col)
        va = va.reshape(B, S, ATTN_KV_HEADS, ATTN_HEAD_DIM)
        qa = qa.reshape(B, S, ATTN_KV_HEADS, ATTN_GROUP, ATTN_HEAD_DIM).transpose(0, 2, 3, 1, 4)
        attn_out = block_attention(qa, ka.transpose(0, 2, 1, 3), va.transpose(0, 2, 1, 3))

        qr = axial_rope(qr.reshape(B, S, RET_HEADS, RET_HEAD_DIM), row, col)
        kr = axial_rope(kr.reshape(B, S, RET_HEADS, RET_HEAD_DIM), row, col) * (RET_HEAD_DIM ** -0.5)
        vr = vr.reshape(B, S, RET_HEADS, RET_HEAD_DIM)
        qf = qr.astype(jnp.float32).transpose(0, 2, 1, 3)
        kf = kr.astype(jnp.float32).transpose(0, 2, 1, 3)
        vf = vr.astype(jnp.float32).transpose(0, 2, 1, 3)
        lg_f = -jnp.exp(ret_decay_fwd.astype(jnp.float32))
        lg_b = -jnp.exp(ret_decay_bwd.astype(jnp.float32))
        ret_f = retention_scan(qf, kf, vf, lg_f, strict=False)
        ret_b = jnp.flip(retention_scan(jnp.flip(qf, 2), jnp.flip(kf, 2), jnp.flip(vf, 2), lg_b, strict=True), 2)
        ret = (ret_f + ret_b).transpose(0, 2, 1, 3)
        mu = jnp.mean(ret, axis=-1, keepdims=True)
        var = jnp.mean(jnp.square(ret - mu), axis=-1, keepdims=True)
        ret = (ret - mu) * lax.rsqrt(var + NORM_EPS) * ret_norm_g.astype(jnp.float32).reshape(RET_HEADS, RET_HEAD_DIM)
        ret_out = (ret.reshape(B, S, RET_WIDTH).astype(x.dtype) * jax.nn.silu(gr))

        mix = jnp.concatenate([attn_out, ret_out], axis=-1) @ w_out
        x = x + gate1[:, None, :] * mix

        h2 = modulate(x, norm2_g, shift2, scale2)
        affinity = jax.nn.softmax((h2 @ w_router).astype(jnp.float32), axis=-1)
        cap = CAPACITY_FACTOR * S // N_EXPERTS
        gate_vals, tok_idx = lax.top_k(affinity.transpose(0, 2, 1), cap)
        xe = jax.vmap(lambda hb, ib: hb[ib])(h2, tok_idx)
        hid = jax.nn.silu(jnp.einsum('becd,edf->becf', xe, w1)) * jnp.einsum('becd,edf->becf', xe, w3)
        ye = jnp.einsum('becf,efd->becd', hid, w2) * gate_vals[..., None].astype(x.dtype)
        moe = jax.vmap(lambda yb, ib: jnp.zeros((S, D), yb.dtype).at[ib.reshape(-1)].add(yb.reshape(-1, D)))(ye, tok_idx)
        x = x + gate2[:, None, :] * moe

    return rmsnorm(x, final_g)
```

```python
import functools
import math

import numpy as np
import jax
import jax.numpy as jnp
from jax import lax
from jax.experimental import pallas as pl
from jax.experimental.pallas import tpu as pltpu

F32 = jnp.float32
BF16 = jnp.bfloat16

GRID_W = 64
ROPE_THETA = 10000.0
NORM_EPS = 1e-6
ATTN_HEAD_DIM = 128
ATTN_KV_HEADS = 4
RET_HEAD_DIM = 256
RET_CHUNK = 128
CAPACITY_FACTOR = 2

LANES = 128
VMEM_LIMIT_BYTES = 56 * 1024 * 1024

NT_DIMS = (((1,), (1,)), ((), ()))
TN_DIMS = (((0,), (0,)), ((), ()))


def _params(*sem):
    return pltpu.CompilerParams(dimension_semantics=sem, vmem_limit_bytes=VMEM_LIMIT_BYTES)


def _pick(dim, pref, align=LANES):
    if dim <= pref:
        return dim
    t = (pref // align) * align
    while t >= align:
        if dim % t == 0:
            return t
        t -= align
    return dim


def _silu(v):
    return v * jax.nn.sigmoid(v)


def _adaln_body(c_ref, w_ref, b_ref, o_ref):
    c = c_ref[...]
    bp = c.shape[0]
    sc = _silu(c)
    hi = sc.astype(BF16).astype(F32)
    lhs = jnp.concatenate([hi, sc - hi], axis=0).astype(BF16)
    acc = jnp.dot(lhs, w_ref[...].astype(BF16), preferred_element_type=F32)
    o_ref[...] = acc[:bp] + acc[bp:] + b_ref[...]


def adaln(c, w_ada, b_ada):
    B, D = c.shape
    N = w_ada.shape[1]
    bp = -(-B // 8) * 8
    cp = jnp.pad(c, ((0, bp - B), (0, 0)))
    tn = _pick(N, 512)
    out = pl.pallas_call(
        _adaln_body,
        grid=(N // tn,),
        in_specs=[pl.BlockSpec((bp, D), lambda j: (0, 0)),
                  pl.BlockSpec((D, tn), lambda j: (0, j)),
                  pl.BlockSpec((1, tn), lambda j: (0, j))],
        out_specs=pl.BlockSpec((bp, tn), lambda j: (0, j)),
        out_shape=jax.ShapeDtypeStruct((bp, N), F32),
        compiler_params=_params("arbitrary"),
        name="adaln",
    )(cp, w_ada, b_ada.reshape(1, N))
    return out[:B]


def _modulated_norm(x, g, shift, scale):
    ms = jnp.mean(x * x, axis=-1, keepdims=True)
    return x * lax.rsqrt(ms + NORM_EPS) * g * (1.0 + scale) + shift


def _modulate_body(x_ref, g_ref, mod_ref, o_ref, *, shift_row, scale_row):
    h = _modulated_norm(x_ref[...], g_ref[...], mod_ref[shift_row:shift_row + 1, :],
                        mod_ref[scale_row:scale_row + 1, :])
    o_ref[...] = h.astype(o_ref.dtype)


def modulate(x, g, mod3, shift_row, scale_row):
    B, S, D = x.shape
    ts = _pick(S, 256, 8)
    return pl.pallas_call(
        functools.partial(_modulate_body, shift_row=shift_row, scale_row=scale_row),
        grid=(B, S // ts),
        in_specs=[pl.BlockSpec((None, ts, D), lambda b, i: (b, i, 0)),
                  pl.BlockSpec((1, D), lambda b, i: (0, 0)),
                  pl.BlockSpec((None, 6, D), lambda b, i: (b, 0, 0))],
        out_specs=pl.BlockSpec((None, ts, D), lambda b, i: (b, i, 0)),
        out_shape=jax.ShapeDtypeStruct((B, S, D), BF16),
        compiler_params=_params("parallel", "parallel"),
        name="modulate",
    )(x, g.reshape(1, D), mod3)


def _proj_attn_body(h_ref, w_ref, g_ref, cs_ref, cos_ref, sin_ref, o_ref):
    acc = jnp.dot(h_ref[...], w_ref[...], preferred_element_type=F32)
    cosv = cos_ref[...]
    sinv = sin_ref[...]
    d = ATTN_HEAD_DIM
    for hh in range(acc.shape[1] // d):
        sl = slice(hh * d, (hh + 1) * d)
        xh = acc[:, sl]
        ms = jnp.mean(xh * xh, axis=-1, keepdims=True)
        y = xh * lax.rsqrt(ms + NORM_EPS) * g_ref[:, sl]
        rot = pltpu.roll(y, d // 2, axis=1)
        o_ref[:, sl] = ((y * cosv + rot * sinv) * cs_ref[:, sl]).astype(o_ref.dtype)


def _proj_ret_body(h_ref, w_ref, cs_ref, cos_ref, sin_ref, o_ref):
    acc = jnp.dot(h_ref[...], w_ref[...], preferred_element_type=F32)
    cosv = cos_ref[...]
    sinv = sin_ref[...]
    d = RET_HEAD_DIM
    for hh in range(acc.shape[1] // d):
        s1 = slice(hh * d, hh * d + d // 2)
        s2 = slice(hh * d + d // 2, (hh + 1) * d)
        x1 = acc[:, s1]
        x2 = acc[:, s2]
        o_ref[:, s1] = ((x1 * cosv - x2 * sinv) * cs_ref[:, s1]).astype(o_ref.dtype)
        o_ref[:, s2] = ((x1 * sinv + x2 * cosv) * cs_ref[:, s2]).astype(o_ref.dtype)


def _proj_plain_body(h_ref, w_ref, o_ref):
    o_ref[...] = jnp.dot(h_ref[...], w_ref[...], preferred_element_type=F32).astype(o_ref.dtype)


def _proj_call(body, h2d, w, rows, tables, S, tn_pref, name):
    T, K = h2d.shape
    N = w.shape[1]
    tm = _pick(S, 1024, 8)
    tn = _pick(N, tn_pref, 2 * LANES)
    per_b = S // tm
    in_specs = [pl.BlockSpec((tm, K), lambda i, j: (i, 0)),
                pl.BlockSpec((K, tn), lambda i, j: (0, j))]
    in_specs += [pl.BlockSpec((1, tn), lambda i, j: (0, j)) for _ in rows]
    in_specs += [pl.BlockSpec((tm, t.shape[1]), lambda i, j: (i % per_b, 0)) for t in tables]
    return pl.pallas_call(
        body,
        grid=(T // tm, N // tn),
        in_specs=in_specs,
        out_specs=pl.BlockSpec((tm, tn), lambda i, j: (i, j)),
        out_shape=jax.ShapeDtypeStruct((T, N), BF16),
        compiler_params=_params("parallel", "arbitrary"),
        name=name,
    )(h2d, w, *rows, *tables)


def _rope_tables(S, d):
    quarter = d // 4
    t = jnp.arange(S)
    row = (t // GRID_W).astype(F32)
    col = (t % GRID_W).astype(F32)
    inv = ROPE_THETA ** (-jnp.arange(quarter, dtype=F32) / quarter)
    ang = jnp.concatenate([row[:, None] * inv, col[:, None] * inv], axis=-1)
    return jnp.cos(ang), jnp.sin(ang)


def _deinterleave_cols(w, d):
    K, N = w.shape
    return w.reshape(K, N // d, d // 2, 2).transpose(0, 1, 3, 2).reshape(K, N)


def _deinterleave_vec(g, d):
    return g.reshape(-1, d // 2, 2).transpose(0, 2, 1).reshape(-1)


def _attn_body(q_ref, k_ref, v_ref, o_ref, m_sc, l_sc, acc_sc, *, tk, groups):
    tq = q_ref.shape[0]
    S, dh = k_ref.shape
    q = jnp.concatenate([q_ref[:, g * dh:(g + 1) * dh] for g in range(groups)], axis=0)
    m_sc[...] = jnp.full(m_sc.shape, -jnp.inf, F32)
    l_sc[...] = jnp.zeros(l_sc.shape, F32)
    acc_sc[...] = jnp.zeros(acc_sc.shape, F32)

    def step(j, carry):
        off = pl.multiple_of(j * tk, tk)
        kc = k_ref[pl.ds(off, tk), :]
        vc = v_ref[pl.ds(off, tk), :]
        st = lax.dot_general(kc, q, NT_DIMS, preferred_element_type=F32)
        m_prev = m_sc[...]
        m_new = jnp.maximum(m_prev, jnp.max(st, axis=0, keepdims=True))
        alpha = jnp.exp2(m_prev - m_new)
        p = jnp.exp2(st - m_new)
        l_sc[...] = alpha * l_sc[...] + jnp.sum(p, axis=0, keepdims=True)
        pv = lax.dot_general(vc, p.astype(BF16), TN_DIMS, preferred_element_type=F32)
        acc_sc[...] = acc_sc[...] * alpha + pv
        m_sc[...] = m_new
        return carry

    lax.fori_loop(0, S // tk, step, 0)
    ot = acc_sc[...] / l_sc[...]
    for g in range(groups):
        o_ref[:, g * dh:(g + 1) * dh] = ot[:, g * tq:(g + 1) * tq].T.astype(o_ref.dtype)


def attention(qk, vg, n_heads):
    B, S, _ = qk.shape
    dh = ATTN_HEAD_DIM
    kvh = ATTN_KV_HEADS
    groups = n_heads // kvh
    tq = _pick(S, 256, 8)
    tk = _pick(S, 512, 8)
    gw = groups * dh
    return pl.pallas_call(
        functools.partial(_attn_body, tk=tk, groups=groups),
        grid=(B, kvh, S // tq),
        in_specs=[pl.BlockSpec((None, tq, gw), lambda b, k, i: (b, i, k)),
                  pl.BlockSpec((None, S, dh), lambda b, k, i: (b, 0, n_heads + k)),
                  pl.BlockSpec((None, S, dh), lambda b, k, i: (b, 0, k))],
        out_specs=pl.BlockSpec((None, tq, gw), lambda b, k, i: (b, i, k)),
        out_shape=jax.ShapeDtypeStruct((B, S, n_heads * dh), BF16),
        scratch_shapes=[pltpu.VMEM((1, groups * tq), F32),
                        pltpu.VMEM((1, groups * tq), F32),
                        pltpu.VMEM((dh, groups * tq), F32)],
        compiler_params=_params("parallel", "parallel", "arbitrary"),
        name="attention",
    )(qk, qk, vg)


def _ret_body(dec_ref, q_ref, k_ref, v_ref, gr_ref, g_ref, o_ref, rf_sc, rb_sc, racc_sc, *, chunk):
    C = chunk
    S, dk = q_ref.shape
    n = S // C
    h = pl.program_id(1)
    lgf = -jnp.exp(jnp.full((1, 1), dec_ref[0, h], F32))
    lgb = -jnp.exp(jnp.full((1, 1), dec_ref[1, h], F32))
    idx = lax.broadcasted_iota(jnp.int32, (C, 1), 0).astype(F32)
    zeta_f = jnp.exp(lgf * (C - 1.0 - idx))
    xi_f = jnp.exp(lgf * (idx + 1.0))
    zeta_b = jnp.exp(lgb * idx)
    xi_b = jnp.exp(lgb * (C - idx))
    cd_f = jnp.exp(lgf * C)
    cd_b = jnp.exp(lgb * C)
    diff = (lax.broadcasted_iota(jnp.int32, (C, C), 0) - lax.broadcasted_iota(jnp.int32, (C, C), 1)).astype(F32)
    decay = jnp.where(diff >= 0, jnp.exp(lgf * jnp.maximum(diff, 0.0)), jnp.exp(lgb * jnp.maximum(-diff, 0.0)))

    def kv_update(i, zeta, cd):
        off = pl.multiple_of(i * C, C)
        kz = (k_ref[pl.ds(off, C), :].astype(F32) * zeta).astype(BF16)
        kv = lax.dot_general(kz, v_ref[pl.ds(off, C), :], TN_DIMS, preferred_element_type=F32)
        racc_sc[...] = racc_sc[...] * cd + kv

    racc_sc[...] = jnp.zeros(racc_sc.shape, F32)

    def fstep(i, carry):
        rf_sc[i] = racc_sc[...].astype(BF16)
        kv_update(i, zeta_f, cd_f)
        return carry

    lax.fori_loop(0, n, fstep, 0)
    racc_sc[...] = jnp.zeros(racc_sc.shape, F32)

    def bstep(t, carry):
        i = n - 1 - t
        rb_sc[i] = racc_sc[...].astype(BF16)
        kv_update(i, zeta_b, cd_b)
        return carry

    lax.fori_loop(0, n, bstep, 0)
    gain = g_ref[...]

    def ostep(i, carry):
        off = pl.multiple_of(i * C, C)
        qi = q_ref[pl.ds(off, C), :]
        ki = k_ref[pl.ds(off, C), :]
        vi = v_ref[pl.ds(off, C), :]
        s = lax.dot_general(qi, ki, NT_DIMS, preferred_element_type=F32) * decay
        qf = (qi.astype(F32) * xi_f).astype(BF16)
        qb = (qi.astype(F32) * xi_b).astype(BF16)
        o = (jnp.dot(s.astype(BF16), vi, preferred_element_type=F32)
             + jnp.dot(qf, rf_sc[i], preferred_element_type=F32)
             + jnp.dot(qb, rb_sc[i], preferred_element_type=F32))
        mu = jnp.mean(o, axis=-1, keepdims=True)
        d = o - mu
        var = jnp.mean(d * d, axis=-1, keepdims=True)
        y = d * lax.rsqrt(var + NORM_EPS) * gain
        o_ref[pl.ds(off, C), :] = (y * _silu(gr_ref[pl.ds(off, C), :].astype(F32))).astype(o_ref.dtype)
        return carry

    lax.fori_loop(0, n, ostep, 0)


def retention(qk, vg, decays, gain, n_heads, v_off, g_off):
    B, S, _ = qk.shape
    d = RET_HEAD_DIM
    n = S // RET_CHUNK
    blk = lambda off: pl.BlockSpec((None, S, d), lambda b, h: (b, 0, off + h))
    return pl.pallas_call(
        functools.partial(_ret_body, chunk=RET_CHUNK),
        grid=(B, n_heads),
        in_specs=[pl.BlockSpec(memory_space=pltpu.SMEM),
                  blk(0), blk(n_heads), blk(v_off), blk(g_off),
                  pl.BlockSpec((None, 1, d), lambda b, h: (h, 0, 0))],
        out_specs=blk(0),
        out_shape=jax.ShapeDtypeStruct((B, S, n_heads * d), BF16),
        scratch_shapes=[pltpu.VMEM((n, d, d), BF16), pltpu.VMEM((n, d, d), BF16), pltpu.VMEM((d, d), F32)],
        compiler_params=_params("parallel", "parallel"),
        name="retention",
    )(decays, qk, qk, vg, vg, gain.reshape(n_heads, 1, d))


def _outproj_body(a_ref, r_ref, wa_ref, wr_ref, x_ref, mod_ref, o_ref, *, gate_row):
    acc = (jnp.dot(a_ref[...], wa_ref[...], preferred_element_type=F32)
           + jnp.dot(r_ref[...], wr_ref[...], preferred_element_type=F32))
    o_ref[...] = x_ref[...] + mod_ref[gate_row:gate_row + 1, :] * acc


def outproj(attn_out, ret_out, w_top, w_bot, x2d, mod3, S, gate_row):
    T, Ka = attn_out.shape
    Kr = ret_out.shape[1]
    N = w_top.shape[1]
    tm = _pick(S, 1024, 8)
    tn = _pick(N, 512, 2 * LANES)
    per_b = S // tm
    return pl.pallas_call(
        functools.partial(_outproj_body, gate_row=gate_row),
        grid=(T // tm, N // tn),
        in_specs=[pl.BlockSpec((tm, Ka), lambda i, j: (i, 0)),
                  pl.BlockSpec((tm, Kr), lambda i, j: (i, 0)),
                  pl.BlockSpec((Ka, tn), lambda i, j: (0, j)),
                  pl.BlockSpec((Kr, tn), lambda i, j: (0, j)),
                  pl.BlockSpec((tm, tn), lambda i, j: (i, j)),
                  pl.BlockSpec((None, 6, tn), lambda i, j: (i // per_b, 0, j))],
        out_specs=pl.BlockSpec((tm, tn), lambda i, j: (i, j)),
        out_shape=jax.ShapeDtypeStruct((T, N), F32),
        compiler_params=_params("parallel", "arbitrary"),
        name="outproj",
    )(attn_out, ret_out, w_top, w_bot, x2d, mod3)


def _router_body(x_ref, g_ref, mod_ref, wr_ref, h_ref, aff_ref, *, shift_row, scale_row):
    h = _modulated_norm(x_ref[...], g_ref[...], mod_ref[shift_row:shift_row + 1, :],
                        mod_ref[scale_row:scale_row + 1, :])
    hi = h.astype(BF16)
    h_ref[...] = hi
    lo = (h - hi.astype(F32)).astype(BF16)
    whi = wr_ref[0]
    wlo = wr_ref[1]
    lg = (lax.dot_general(whi, hi, NT_DIMS, preferred_element_type=F32)
          + lax.dot_general(whi, lo, NT_DIMS, preferred_element_type=F32)
          + lax.dot_general(wlo, hi, NT_DIMS, preferred_element_type=F32))
    e = jnp.exp(lg - jnp.max(lg, axis=0, keepdims=True))
    aff_ref[...] = e / jnp.sum(e, axis=0, keepdims=True)


def router(x1, g, mod3, w_router, shift_row, scale_row):
    B, S, D = x1.shape
    E = w_router.shape[1]
    wt = w_router.T
    whi = wt.astype(BF16)
    wsplit = jnp.stack([whi, (wt - whi.astype(F32)).astype(BF16)])
    ts = _pick(S, 256)
    return pl.pallas_call(
        functools.partial(_router_body, shift_row=shift_row, scale_row=scale_row),
        grid=(B, S // ts),
        in_specs=[pl.BlockSpec((None, ts, D), lambda b, i: (b, i, 0)),
                  pl.BlockSpec((1, D), lambda b, i: (0, 0)),
                  pl.BlockSpec((None, 6, D), lambda b, i: (b, 0, 0)),
                  pl.BlockSpec((2, E, D), lambda b, i: (0, 0, 0))],
        out_specs=[pl.BlockSpec((None, ts, D), lambda b, i: (b, i, 0)),
                   pl.BlockSpec((None, E, ts), lambda b, i: (b, 0, i))],
        out_shape=[jax.ShapeDtypeStruct((B, S, D), BF16), jax.ShapeDtypeStruct((B, E, S), F32)],
        compiler_params=_params("parallel", "parallel"),
        name="router",
    )(x1, g.reshape(1, D), mod3, wsplit)


def _cumsum_lanes(x):
    rows, S = x.shape
    tri = (lax.broadcasted_iota(jnp.int32, (LANES, LANES), 0)
           <= lax.broadcasted_iota(jnp.int32, (LANES, LANES), 1))
    tri = jnp.where(tri, 1.0, 0.0).astype(BF16)
    carry = jnp.zeros((rows, 1), F32)
    outs = []
    for c in range(S // LANES):
        inc = jnp.dot(x[:, c * LANES:(c + 1) * LANES].astype(BF16), tri, preferred_element_type=F32) + carry
        outs.append(inc)
        carry = inc[:, LANES - 1:LANES]
    return jnp.concatenate(outs, axis=1)


def _select_body(aff_ref, pos_ref, *, cap):
    bits = pltpu.bitcast(aff_ref[...], jnp.int32)
    E = bits.shape[0]

    def count(mask):
        return jnp.sum(jnp.where(mask, 1.0, 0.0), axis=1, keepdims=True)

    def search(i, t):
        cand = t | jnp.left_shift(jnp.int32(1), 30 - i)
        return jnp.where(count(bits >= cand) >= cap, cand, t)

    thr = lax.fori_loop(0, 31, search, jnp.zeros((E, 1), jnp.int32))
    gt = bits > thr
    eq = bits == thr
    need = cap - count(gt)
    eq_rank = _cumsum_lanes(jnp.where(eq, 1.0, 0.0))
    sel = gt | (eq & (eq_rank <= need))
    pos = _cumsum_lanes(jnp.where(sel, 1.0, 0.0)) - 1.0
    pos_ref[...] = jnp.where(sel, pos, -1.0).astype(jnp.int32)


def select(aff_t, cap):
    B, E, S = aff_t.shape
    return pl.pallas_call(
        functools.partial(_select_body, cap=cap),
        grid=(B,),
        in_specs=[pl.BlockSpec((None, E, S), lambda b: (b, 0, 0))],
        out_specs=pl.BlockSpec((None, E, S), lambda b: (b, 0, 0)),
        out_shape=jax.ShapeDtypeStruct((B, E, S), jnp.int32),
        compiler_params=_params("parallel"),
        name="select",
    )(aff_t)


def _gather_body(pos_ref, aff_ref, h_ref, xe_ref, gate_ref, acc_sc, gacc_sc, *, cap):
    k = pl.program_id(2)

    @pl.when(k == 0)
    def _():
        acc_sc[...] = jnp.zeros(acc_sc.shape, F32)
        gacc_sc[...] = jnp.zeros(gacc_sc.shape, F32)

    ts = pos_ref.shape[1]
    onehot = pos_ref[...] == lax.broadcasted_iota(jnp.int32, (cap, ts), 0)
    acc_sc[...] += jnp.dot(jnp.where(onehot, 1.0, 0.0).astype(BF16), h_ref[...], preferred_element_type=F32)
    gacc_sc[...] += jnp.sum(jnp.where(onehot, aff_ref[...], 0.0), axis=1, keepdims=True)

    @pl.when(k == pl.num_programs(2) - 1)
    def _():
        xe_ref[...] = acc_sc[...].astype(xe_ref.dtype)
        gate_ref[...] = jnp.broadcast_to(gacc_sc[...], gate_ref.shape)


def gather_tokens(pos, aff_t, h2, cap):
    B, E, S = pos.shape
    D = h2.shape[2]
    ts = _pick(S, 512)
    row = lambda a: a.reshape(B * E, 1, S)
    return pl.pallas_call(
        functools.partial(_gather_body, cap=cap),
        grid=(B, E, S // ts),
        in_specs=[pl.BlockSpec((None, 1, ts), lambda b, e, k: (b * E + e, 0, k)),
                  pl.BlockSpec((None, 1, ts), lambda b, e, k: (b * E + e, 0, k)),
                  pl.BlockSpec((None, ts, D), lambda b, e, k: (b, k, 0))],
        out_specs=[pl.BlockSpec((None, cap, D), lambda b, e, k: (e, b, 0)),
                   pl.BlockSpec((None, cap, LANES), lambda b, e, k: (e, b, 0))],
        out_shape=[jax.ShapeDtypeStruct((E, B * cap, D), BF16),
                   jax.ShapeDtypeStruct((E, B * cap, LANES), F32)],
        scratch_shapes=[pltpu.VMEM((cap, D), F32), pltpu.VMEM((cap, 1), F32)],
        compiler_params=_params("parallel", "parallel", "arbitrary"),
        name="gather_tokens",
    )(row(pos), row(aff_t), h2)


def _up_body(x_ref, w1_ref, w3_ref, o_ref):
    x = x_ref[...]
    a = jnp.dot(x, w1_ref[...].astype(BF16), preferred_element_type=F32)
    b = jnp.dot(x, w3_ref[...].astype(BF16), preferred_element_type=F32)
    o_ref[...] = (_silu(a) * b).astype(o_ref.dtype)


def expert_up(xe, w1, w3):
    E, M, D = xe.shape
    Fd = w1.shape[2]
    tm = _pick(M, 1024, 8)
    tf = _pick(Fd, 256, 2 * LANES)
    return pl.pallas_call(
        _up_body,
        grid=(E, M // tm, Fd // tf),
        in_specs=[pl.BlockSpec((None, tm, D), lambda e, m, f: (e, m, 0)),
                  pl.BlockSpec((None, D, tf), lambda e, m, f: (e, 0, f)),
                  pl.BlockSpec((None, D, tf), lambda e, m, f: (e, 0, f))],
        out_specs=pl.BlockSpec((None, tm, tf), lambda e, m, f: (e, m, f)),
        out_shape=jax.ShapeDtypeStruct((E, M, Fd), BF16),
        compiler_params=_params("parallel", "parallel", "arbitrary"),
        name="expert_up",
    )(xe, w1, w3)


def _down_body(h_ref, w2_ref, gate_ref, o_ref):
    y = jnp.dot(h_ref[...], w2_ref[...].astype(BF16), preferred_element_type=F32)
    o_ref[...] = (y * gate_ref[:, 0:1]).astype(o_ref.dtype)


def expert_down(hid, w2, gate):
    E, M, Fd = hid.shape
    D = w2.shape[2]
    tm = _pick(M, 1024, 8)
    tn = _pick(D, 512, 2 * LANES)
    return pl.pallas_call(
        _down_body,
        grid=(E, M // tm, D // tn),
        in_specs=[pl.BlockSpec((None, tm, Fd), lambda e, m, n: (e, m, 0)),
                  pl.BlockSpec((None, Fd, tn), lambda e, m, n: (e, 0, n)),
                  pl.BlockSpec((None, tm, LANES), lambda e, m, n: (e, m, 0))],
        out_specs=pl.BlockSpec((None, tm, tn), lambda e, m, n: (e, m, n)),
        out_shape=jax.ShapeDtypeStruct((E, M, D), BF16),
        compiler_params=_params("parallel", "parallel", "arbitrary"),
        name="expert_down",
    )(hid, w2, gate)


def _combine_body(pos_ref, ye_ref, x_ref, mod_ref, o_ref, *, gate_row):
    E, cap, _ = ye_ref.shape
    tm = pos_ref.shape[1]
    slot = lax.broadcasted_iota(jnp.int32, (cap, tm), 0)
    acc = jnp.zeros(o_ref.shape, F32)
    for e in range(E):
        onehot_t = jnp.where(pos_ref[e:e + 1, :] == slot, 1.0, 0.0).astype(BF16)
        acc += lax.dot_general(onehot_t, ye_ref[e], TN_DIMS, preferred_element_type=F32)
    o_ref[...] = x_ref[...] + mod_ref[gate_row:gate_row + 1, :] * acc


def combine(pos, ye, x1, mod3, cap, gate_row):
    B, E, S = pos.shape
    D = x1.shape[2]
    tm = _pick(S, 256)
    tn = _pick(D, 512, 2 * LANES)
    ye4 = ye.reshape(E, B, cap, D)
    return pl.pallas_call(
        functools.partial(_combine_body, gate_row=gate_row),
        grid=(B, D // tn, S // tm),
        in_specs=[pl.BlockSpec((None, E, tm), lambda b, n, t: (b, 0, t)),
                  pl.BlockSpec((E, None, cap, tn), lambda b, n, t: (0, b, 0, n)),
                  pl.BlockSpec((None, tm, tn), lambda b, n, t: (b, t, n)),
                  pl.BlockSpec((None, 6, tn), lambda b, n, t: (b, 0, n))],
        out_specs=pl.BlockSpec((None, tm, tn), lambda b, n, t: (b, t, n)),
        out_shape=jax.ShapeDtypeStruct((B, S, D), F32),
        compiler_params=_params("parallel", "parallel", "arbitrary"),
        name="combine",
    )(pos, ye4, x1, mod3)


def _final_norm_body(x_ref, g_ref, o_ref):
    x = x_ref[...]
    ms = jnp.mean(x * x, axis=-1, keepdims=True)
    o_ref[...] = x * lax.rsqrt(ms + NORM_EPS) * g_ref[...]


def final_norm(x2d, g):
    T, D = x2d.shape
    tm = _pick(T, 256, 8)
    return pl.pallas_call(
        _final_norm_body,
        grid=(T // tm,),
        in_specs=[pl.BlockSpec((tm, D), lambda i: (i, 0)), pl.BlockSpec((1, D), lambda i: (0, 0))],
        out_specs=pl.BlockSpec((tm, D), lambda i: (i, 0)),
        out_shape=jax.ShapeDtypeStruct((T, D), F32),
        compiler_params=_params("parallel"),
        name="final_norm",
    )(x2d, g.reshape(1, D))


def kernel(x, c, w_ada, b_ada, norm1_g, w_in, attn_q_norm_g, attn_k_norm_g, ret_decay_fwd, ret_decay_bwd,
           ret_norm_g, w_out, norm2_g, w_router, w1, w3, w2, final_g):
    B, S, D = x.shape
    T = B * S
    mix = w_out.shape[0]
    attn_w = mix // 2
    ret_w = mix - attn_w
    ah = attn_w // ATTN_HEAD_DIM
    kv_w = ATTN_KV_HEADS * ATTN_HEAD_DIM
    rh = ret_w // RET_HEAD_DIM
    E = w_router.shape[1]
    cap = CAPACITY_FACTOR * S // E

    o1 = attn_w + kv_w
    o2 = o1 + kv_w
    o3 = o2 + 2 * ret_w
    w_a = _deinterleave_cols(w_in[:, :o1], ATTN_HEAD_DIM).astype(BF16)
    w_b = _deinterleave_cols(w_in[:, o2:o3], RET_HEAD_DIM).astype(BF16)
    w_c = jnp.concatenate([w_in[:, o1:o2], w_in[:, o3:]], axis=1).astype(BF16)
    gq = _deinterleave_vec(attn_q_norm_g, ATTN_HEAD_DIM)
    gk = _deinterleave_vec(attn_k_norm_g, ATTN_HEAD_DIM)
    gain_a = jnp.concatenate([jnp.tile(gq, ah), jnp.tile(gk, ATTN_KV_HEADS)]).reshape(1, o1)
    q_scale = math.log2(math.e) / math.sqrt(ATTN_HEAD_DIM)
    cs_a = jnp.concatenate([jnp.full((attn_w,), q_scale, F32), jnp.ones((kv_w,), F32)]).reshape(1, o1)
    cs_b = jnp.concatenate([jnp.ones((ret_w,), F32), jnp.full((ret_w,), RET_HEAD_DIM ** -0.5, F32)]).reshape(1, 2 * ret_w)
    cos_a, sin_a = _rope_tables(S, ATTN_HEAD_DIM)
    cos_a2 = jnp.concatenate([cos_a, cos_a], axis=1)
    sin_a2 = jnp.concatenate([-sin_a, sin_a], axis=1)
    cos_r, sin_r = _rope_tables(S, RET_HEAD_DIM)

    mod3 = adaln(c, w_ada, b_ada).reshape(B, 6, D)

    h = modulate(x, norm1_g, mod3, 0, 1).reshape(T, D)
    qk_a = _proj_call(_proj_attn_body, h, w_a, [gain_a, cs_a], [cos_a2, sin_a2], S, 512, "proj_attn_qk")
    qk_r = _proj_call(_proj_ret_body, h, w_b, [cs_b], [cos_r, sin_r], S, 1024, "proj_ret_qk")
    vg = _proj_call(_proj_plain_body, h, w_c, [], [], S, 768, "proj_vg")
    attn_out = attention(qk_a.reshape(B, S, o1), vg.reshape(B, S, -1), ah)
    decays = jnp.stack([ret_decay_fwd, ret_decay_bwd]).astype(F32)
    v_off = kv_w // RET_HEAD_DIM
    ret_out = retention(qk_r.reshape(B, S, 2 * ret_w), vg.reshape(B, S, -1), decays, ret_norm_g.astype(F32),
                        rh, v_off, v_off + rh)
    w_o = w_out.astype(BF16)
    x1 = outproj(attn_out.reshape(T, attn_w), ret_out.reshape(T, ret_w), w_o[:attn_w], w_o[attn_w:],
                 x.reshape(T, D), mod3, S, 2).reshape(B, S, D)

    h2, aff_t = router(x1, norm2_g, mod3, w_router, 3, 4)
    pos = select(aff_t, cap)
    xe, gate = gather_tokens(pos, aff_t, h2, cap)
    ye = expert_down(expert_up(xe, w1, w3), w2, gate)
    x2 = combine(pos, ye, x1, mod3, cap, 5)
    return final_norm(x2.reshape(T, D), final_g).reshape(B, S, D)
```

```python
import functools
import math

import numpy as np
import jax
import jax.numpy as jnp
from jax import lax
from jax.experimental import pallas as pl
from jax.experimental.pallas import tpu as pltpu

F32 = jnp.float32
BF16 = jnp.bfloat16

GRID_W = 64
ROPE_THETA = 10000.0
NORM_EPS = 1e-6
ATTN_HEAD_DIM = 128
ATTN_KV_HEADS = 4
RET_HEAD_DIM = 256
RET_CHUNK = 128
CAPACITY_FACTOR = 2

LANES = 128
VMEM_LIMIT_BYTES = 56 * 1024 * 1024

NT_DIMS = (((1,), (1,)), ((), ()))
TN_DIMS = (((0,), (0,)), ((), ()))


def _params(*sem):
    return pltpu.CompilerParams(dimension_semantics=sem, vmem_limit_bytes=VMEM_LIMIT_BYTES)


def _pick(dim, pref, align=LANES):
    if dim <= pref:
        return dim
    t = (pref // align) * align
    while t >= align:
        if dim % t == 0:
            return t
        t -= align
    return dim


def _silu(v):
    return v * jax.nn.sigmoid(v)


def _adaln_body(c_ref, w_ref, b_ref, o_ref):
    c = c_ref[...]
    bp = c.shape[0]
    sc = _silu(c)
    hi = sc.astype(BF16).astype(F32)
    lhs = jnp.concatenate([hi, sc - hi], axis=0).astype(BF16)
    acc = jnp.dot(lhs, w_ref[...].astype(BF16), preferred_element_type=F32)
    o_ref[...] = acc[:bp] + acc[bp:] + b_ref[...]


def adaln(c, w_ada, b_ada):
    B, D = c.shape
    N = w_ada.shape[1]
    bp = -(-B // 8) * 8
    cp = jnp.pad(c, ((0, bp - B), (0, 0)))
    tn = _pick(N, 512)
    out = pl.pallas_call(
        _adaln_body,
        grid=(N // tn,),
        in_specs=[pl.BlockSpec((bp, D), lambda j: (0, 0)),
                  pl.BlockSpec((D, tn), lambda j: (0, j)),
                  pl.BlockSpec((1, tn), lambda j: (0, j))],
        out_specs=pl.BlockSpec((bp, tn), lambda j: (0, j)),
        out_shape=jax.ShapeDtypeStruct((bp, N), F32),
        compiler_params=_params("arbitrary"),
        name="adaln",
    )(cp, w_ada, b_ada.reshape(1, N))
    return out[:B]


def _modulated_norm(x, g, shift, scale):
    ms = jnp.mean(x * x, axis=-1, keepdims=True)
    return x * lax.rsqrt(ms + NORM_EPS) * g * (1.0 + scale) + shift


def _modulate_body(x_ref, g_ref, mod_ref, o_ref, *, shift_row, scale_row):
    h = _modulated_norm(x_ref[...], g_ref[...], mod_ref[shift_row:shift_row + 1, :],
                        mod_ref[scale_row:scale_row + 1, :])
    o_ref[...] = h.astype(o_ref.dtype)


def modulate(x, g, mod3, shift_row, scale_row):
    B, S, D = x.shape
    ts = _pick(S, 256, 8)
    return pl.pallas_call(
        functools.partial(_modulate_body, shift_row=shift_row, scale_row=scale_row),
        grid=(B, S // ts),
        in_specs=[pl.BlockSpec((None, ts, D), lambda b, i: (b, i, 0)),
                  pl.BlockSpec((1, D), lambda b, i: (0, 0)),
                  pl.BlockSpec((None, 6, D), lambda b, i: (b, 0, 0))],
        out_specs=pl.BlockSpec((None, ts, D), lambda b, i: (b, i, 0)),
        out_shape=jax.ShapeDtypeStruct((B, S, D), BF16),
        compiler_params=_params("parallel", "parallel"),
        name="modulate",
    )(x, g.reshape(1, D), mod3)


def _proj_attn_body(h_ref, w_ref, g_ref, cs_ref, cos_ref, sin_ref, o_ref):
    acc = jnp.dot(h_ref[...], w_ref[...], preferred_element_type=F32)
    cosv = cos_ref[...]
    sinv = sin_ref[...]
    d = ATTN_HEAD_DIM
    for hh in range(acc.shape[1] // d):
        sl = slice(hh * d, (hh + 1) * d)
        xh = acc[:, sl]
        ms = jnp.mean(xh * xh, axis=-1, keepdims=True)
        y = xh * lax.rsqrt(ms + NORM_EPS) * g_ref[:, sl]
        rot = pltpu.roll(y, d // 2, axis=1)
        o_ref[:, sl] = ((y * cosv + rot * sinv) * cs_ref[:, sl]).astype(o_ref.dtype)


def _proj_ret_body(h_ref, w_ref, cs_ref, cos_ref, sin_ref, o_ref):
    acc = jnp.dot(h_ref[...], w_ref[...], preferred_element_type=F32)
    cosv = cos_ref[...]
    sinv = sin_ref[...]
    d = RET_HEAD_DIM
    for hh in range(acc.shape[1] // d):
        s1 = slice(hh * d, hh * d + d // 2)
        s2 = slice(hh * d + d // 2, (hh + 1) * d)
        x1 = acc[:, s1]
        x2 = acc[:, s2]
        o_ref[:, s1] = ((x1 * cosv - x2 * sinv) * cs_ref[:, s1]).astype(o_ref.dtype)
        o_ref[:, s2] = ((x1 * sinv + x2 * cosv) * cs_ref[:, s2]).astype(o_ref.dtype)


def _proj_plain_body(h_ref, w_ref, o_ref):
    o_ref[...] = jnp.dot(h_ref[...], w_ref[...], preferred_element_type=F32).astype(o_ref.dtype)


def _proj_call(body, h2d, w, rows, tables, S, tn_pref, name):
    T, K = h2d.shape
    N = w.shape[1]
    tm = _pick(S, 1024, 8)
    tn = _pick(N, tn_pref, 2 * LANES)
    per_b = S // tm
    in_specs = [pl.BlockSpec((tm, K), lambda i, j: (i, 0)),
                pl.BlockSpec((K, tn), lambda i, j: (0, j))]
    in_specs += [pl.BlockSpec((1, tn), lambda i, j: (0, j)) for _ in rows]
    in_specs += [pl.BlockSpec((tm, t.shape[1]), lambda i, j: (i % per_b, 0)) for t in tables]
    return pl.pallas_call(
        body,
        grid=(T // tm, N // tn),
        in_specs=in_specs,
        out_specs=pl.BlockSpec((tm, tn), lambda i, j: (i, j)),
        out_shape=jax.ShapeDtypeStruct((T, N), BF16),
        compiler_params=_params("parallel", "arbitrary"),
        name=name,
    )(h2d, w, *rows, *tables)


def _rope_tables(S, d):
    quarter = d // 4
    t = np.arange(S)
    inv = ROPE_THETA ** (-np.arange(quarter, dtype=np.float64) / quarter)
    ang = np.concatenate([(t // GRID_W)[:, None] * inv, (t % GRID_W)[:, None] * inv], axis=-1)
    return np.cos(ang).astype(np.float32), np.sin(ang).astype(np.float32)


def _deinterleave_vec(g, d):
    return g.reshape(-1, d // 2, 2).transpose(0, 2, 1).reshape(-1)


PERM_BLOCK = 256


def _deinterleave_matrix(d):
    src = np.concatenate([h * d + np.concatenate([np.arange(0, d, 2), np.arange(1, d, 2)])
                          for h in range(PERM_BLOCK // d)])
    p = np.zeros((PERM_BLOCK, PERM_BLOCK), np.float32)
    p[src, np.arange(PERM_BLOCK)] = 1.0
    return jnp.asarray(p, BF16)


def _cast_perm_body(w_ref, p_ref, o_ref):
    o_ref[...] = jnp.dot(w_ref[...].astype(BF16), p_ref[...], preferred_element_type=F32).astype(o_ref.dtype)


def _cast_body(w_ref, o_ref):
    o_ref[...] = w_ref[...].astype(o_ref.dtype)


def cast_cols(w, n_out, bw, in_block, perm, name):
    K = w.shape[0]
    tr = _pick(K, 2048, 8)
    in_specs = [pl.BlockSpec((tr, bw), lambda i, j: (i, in_block(j)))]
    args = [w]
    if perm is not None:
        in_specs.append(pl.BlockSpec((bw, bw), lambda i, j: (0, 0)))
        args.append(perm)
    return pl.pallas_call(
        _cast_body if perm is None else _cast_perm_body,
        grid=(K // tr, n_out // bw),
        in_specs=in_specs,
        out_specs=pl.BlockSpec((tr, bw), lambda i, j: (i, j)),
        out_shape=jax.ShapeDtypeStruct((K, n_out), BF16),
        compiler_params=_params("parallel", "parallel"),
        name=name,
    )(*args)


ATTN_TQ = 256
ATTN_TK = 512
ONES_ROWS = 16
ATTN_SAFE_EXP = 50.0


def _attn_body(q_ref, k_ref, v_ref, o_ref, vt_sc, qt_sc, kmax_sc, m_sc, acc_sc, *, tk, groups):
    tq = q_ref.shape[0]
    S, dh = k_ref.shape
    nk = S // tk

    @pl.when(pl.program_id(2) == 0)
    def _():
        for j in range(nk):
            vt_sc[j, :dh, :] = v_ref[j * tk:(j + 1) * tk, :].astype(F32).T.astype(BF16)
            vt_sc[j, dh:, :] = jnp.ones((ONES_ROWS, tk), BF16)
        kf = k_ref[...].astype(F32)
        ksq = jnp.max(jnp.sum(kf * kf, axis=1, keepdims=True), axis=0, keepdims=True)
        kmax_sc[...] = jnp.broadcast_to(jnp.sqrt(ksq), kmax_sc.shape)

    for g in range(groups):
        qt = q_ref[:, g * dh:(g + 1) * dh].astype(F32).T
        qt_sc[:, g * tq:(g + 1) * tq] = qt.astype(BF16)
        m_sc[:, g * tq:(g + 1) * tq] = jnp.sqrt(jnp.sum(qt * qt, axis=0, keepdims=True)) * kmax_sc[:, 0:1]
    acc_sc[...] = jnp.zeros(acc_sc.shape, F32)
    bounded = jnp.max(m_sc[...]) <= ATTN_SAFE_EXP

    @pl.when(bounded)
    def _():
        def step(j, carry):
            off = pl.multiple_of(j * tk, tk)
            st = jnp.dot(k_ref[pl.ds(off, tk), :], qt_sc[...], preferred_element_type=F32)
            p = jnp.exp2(st - m_sc[...]).astype(BF16)
            acc_sc[...] += jnp.dot(vt_sc[j], p, preferred_element_type=F32)
            return carry

        lax.fori_loop(0, nk, step, 0, unroll=True)

    @pl.when(jnp.logical_not(bounded))
    def _():
        m_sc[...] = jnp.full(m_sc.shape, -jnp.inf, F32)

        def step(j, carry):
            off = pl.multiple_of(j * tk, tk)
            st = jnp.dot(k_ref[pl.ds(off, tk), :], qt_sc[...], preferred_element_type=F32)
            m_prev = m_sc[...]
            m_new = jnp.maximum(m_prev, jnp.max(st, axis=0, keepdims=True))
            p = jnp.exp2(st - m_new).astype(BF16)
            acc_sc[...] = acc_sc[...] * jnp.exp2(m_prev - m_new) + jnp.dot(vt_sc[j], p, preferred_element_type=F32)
            m_sc[...] = m_new
            return carry

        lax.fori_loop(0, nk, step, 0)

    ot = acc_sc[:dh, :] / acc_sc[dh:dh + 1, :]
    for g in range(groups):
        o_ref[:, g * dh:(g + 1) * dh] = ot[:, g * tq:(g + 1) * tq].T.astype(o_ref.dtype)


def attention(qk, vg, n_heads):
    B, S, _ = qk.shape
    dh = ATTN_HEAD_DIM
    kvh = ATTN_KV_HEADS
    groups = n_heads // kvh
    tq = _pick(S, ATTN_TQ, 8)
    tk = _pick(S, ATTN_TK, 8)
    gw = groups * dh
    return pl.pallas_call(
        functools.partial(_attn_body, tk=tk, groups=groups),
        grid=(B, kvh, S // tq),
        in_specs=[pl.BlockSpec((None, tq, gw), lambda b, k, i: (b, i, k)),
                  pl.BlockSpec((None, S, dh), lambda b, k, i: (b, 0, n_heads + k)),
                  pl.BlockSpec((None, S, dh), lambda b, k, i: (b, 0, k))],
        out_specs=pl.BlockSpec((None, tq, gw), lambda b, k, i: (b, i, k)),
        out_shape=jax.ShapeDtypeStruct((B, S, n_heads * dh), BF16),
        scratch_shapes=[pltpu.VMEM((S // tk, dh + ONES_ROWS, tk), BF16),
                        pltpu.VMEM((dh, groups * tq), BF16),
                        pltpu.VMEM((1, LANES), F32),
                        pltpu.VMEM((1, groups * tq), F32),
                        pltpu.VMEM((dh + ONES_ROWS, groups * tq), F32)],
        compiler_params=_params("arbitrary", "arbitrary", "arbitrary"),
        name="attention",
    )(qk, qk, vg)


RET_UNROLL = 8


def _ret_body(dec_ref, q_ref, k_ref, v_ref, gr_ref, g_ref, o_ref, rf_sc, rb_sc, racc_sc, *, chunk):
    C = chunk
    S, dk = q_ref.shape
    n = S // C
    h = pl.program_id(1)
    lgf = -jnp.exp(jnp.full((1, 1), dec_ref[0, h], F32))
    lgb = -jnp.exp(jnp.full((1, 1), dec_ref[1, h], F32))
    idx = lax.broadcasted_iota(jnp.int32, (C, 1), 0).astype(F32)
    zeta_f = jnp.exp(lgf * (C - 1.0 - idx))
    xi_f = jnp.exp(lgf * (idx + 1.0))
    zeta_b = jnp.exp(lgb * idx)
    xi_b = jnp.exp(lgb * (C - idx))
    cd_f = jnp.exp(lgf * C)
    cd_b = jnp.exp(lgb * C)
    diff = (lax.broadcasted_iota(jnp.int32, (C, C), 0) - lax.broadcasted_iota(jnp.int32, (C, C), 1)).astype(F32)
    decay = jnp.where(diff >= 0, jnp.exp(lgf * jnp.maximum(diff, 0.0)), jnp.exp(lgb * jnp.maximum(-diff, 0.0)))

    def kv_update(i, zeta, cd):
        off = pl.multiple_of(i * C, C)
        kz = (k_ref[pl.ds(off, C), :].astype(F32) * zeta).astype(BF16)
        kv = lax.dot_general(kz, v_ref[pl.ds(off, C), :], TN_DIMS, preferred_element_type=F32)
        racc_sc[...] = racc_sc[...] * cd + kv

    racc_sc[...] = jnp.zeros(racc_sc.shape, F32)

    def fstep(i, carry):
        rf_sc[i] = racc_sc[...].astype(BF16)
        kv_update(i, zeta_f, cd_f)
        return carry

    lax.fori_loop(0, n, fstep, 0, unroll=RET_UNROLL)
    racc_sc[...] = jnp.zeros(racc_sc.shape, F32)

    def bstep(t, carry):
        i = n - 1 - t
        rb_sc[i] = racc_sc[...].astype(BF16)
        kv_update(i, zeta_b, cd_b)
        return carry

    lax.fori_loop(0, n, bstep, 0, unroll=RET_UNROLL)
    gain = g_ref[...]

    def ostep(i, carry):
        off = pl.multiple_of(i * C, C)
        qi = q_ref[pl.ds(off, C), :]
        ki = k_ref[pl.ds(off, C), :]
        vi = v_ref[pl.ds(off, C), :]
        s = lax.dot_general(qi, ki, NT_DIMS, preferred_element_type=F32) * decay
        qf = (qi.astype(F32) * xi_f).astype(BF16)
        qb = (qi.astype(F32) * xi_b).astype(BF16)
        o = (jnp.dot(s.astype(BF16), vi, preferred_element_type=F32)
             + jnp.dot(qf, rf_sc[i], preferred_element_type=F32)
             + jnp.dot(qb, rb_sc[i], preferred_element_type=F32))
        mu = jnp.mean(o, axis=-1, keepdims=True)
        d = o - mu
        var = jnp.mean(d * d, axis=-1, keepdims=True)
        y = d * lax.rsqrt(var + NORM_EPS) * gain
        o_ref[pl.ds(off, C), :] = (y * _silu(gr_ref[pl.ds(off, C), :].astype(F32))).astype(o_ref.dtype)
        return carry

    lax.fori_loop(0, n, ostep, 0, unroll=RET_UNROLL)


def retention(qk, vg, decays, gain, n_heads, v_off, g_off):
    B, S, _ = qk.shape
    d = RET_HEAD_DIM
    n = S // RET_CHUNK
    blk = lambda off: pl.BlockSpec((None, S, d), lambda b, h: (b, 0, off + h))
    return pl.pallas_call(
        functools.partial(_ret_body, chunk=RET_CHUNK),
        grid=(B, n_heads),
        in_specs=[pl.BlockSpec(memory_space=pltpu.SMEM),
                  blk(0), blk(n_heads), blk(v_off), blk(g_off),
                  pl.BlockSpec((None, 1, d), lambda b, h: (h, 0, 0))],
        out_specs=blk(0),
        out_shape=jax.ShapeDtypeStruct((B, S, n_heads * d), BF16),
        scratch_shapes=[pltpu.VMEM((n, d, d), BF16), pltpu.VMEM((n, d, d), BF16), pltpu.VMEM((d, d), F32)],
        compiler_params=_params("parallel", "parallel"),
        name="retention",
    )(decays, qk, qk, vg, vg, gain.reshape(n_heads, 1, d))


def _outproj_body(a_ref, r_ref, wa_ref, wr_ref, x_ref, mod_ref, o_ref, *, gate_row):
    acc = (jnp.dot(a_ref[...], wa_ref[...], preferred_element_type=F32)
           + jnp.dot(r_ref[...], wr_ref[...], preferred_element_type=F32))
    o_ref[...] = x_ref[...] + mod_ref[gate_row:gate_row + 1, :] * acc


def outproj(attn_out, ret_out, w_o, x2d, mod3, S, gate_row):
    T, Ka = attn_out.shape
    Kr = ret_out.shape[1]
    assert Ka == Kr and w_o.shape[0] == Ka + Kr
    N = w_o.shape[1]
    tm = _pick(S, 1024, 8)
    tn = _pick(N, 512, 2 * LANES)
    per_b = S // tm
    return pl.pallas_call(
        functools.partial(_outproj_body, gate_row=gate_row),
        grid=(T // tm, N // tn),
        in_specs=[pl.BlockSpec((tm, Ka), lambda i, j: (i, 0)),
                  pl.BlockSpec((tm, Kr), lambda i, j: (i, 0)),
                  pl.BlockSpec((Ka, tn), lambda i, j: (0, j)),
                  pl.BlockSpec((Kr, tn), lambda i, j: (1, j)),
                  pl.BlockSpec((tm, tn), lambda i, j: (i, j)),
                  pl.BlockSpec((None, 6, tn), lambda i, j: (i // per_b, 0, j))],
        out_specs=pl.BlockSpec((tm, tn), lambda i, j: (i, j)),
        out_shape=jax.ShapeDtypeStruct((T, N), F32),
        compiler_params=_params("parallel", "arbitrary"),
        name="outproj",
    )(attn_out, ret_out, w_o, w_o, x2d, mod3)


def _router_body(x_ref, g_ref, mod_ref, wr_ref, h_ref, aff_ref, *, shift_row, scale_row):
    h = _modulated_norm(x_ref[...], g_ref[...], mod_ref[shift_row:shift_row + 1, :],
                        mod_ref[scale_row:scale_row + 1, :])
    h_ref[...] = h
    hi = h.astype(BF16)
    lo = (h - hi.astype(F32)).astype(BF16)
    whi = wr_ref[0]
    wlo = wr_ref[1]
    lg = (lax.dot_general(whi, hi, NT_DIMS, preferred_element_type=F32)
          + lax.dot_general(whi, lo, NT_DIMS, preferred_element_type=F32)
          + lax.dot_general(wlo, hi, NT_DIMS, preferred_element_type=F32))
    e = jnp.exp(lg - jnp.max(lg, axis=0, keepdims=True))
    aff_ref[...] = e / jnp.sum(e, axis=0, keepdims=True)


def router(x1, g, mod3, w_router, shift_row, scale_row):
    B, S, D = x1.shape
    E = w_router.shape[1]
    wt = w_router.T
    whi = wt.astype(BF16)
    wsplit = jnp.stack([whi, (wt - whi.astype(F32)).astype(BF16)])
    ts = _pick(S, 256)
    return pl.pallas_call(
        functools.partial(_router_body, shift_row=shift_row, scale_row=scale_row),
        grid=(B, S // ts),
        in_specs=[pl.BlockSpec((None, ts, D), lambda b, i: (b, i, 0)),
                  pl.BlockSpec((1, D), lambda b, i: (0, 0)),
                  pl.BlockSpec((None, 6, D), lambda b, i: (b, 0, 0)),
                  pl.BlockSpec((2, E, D), lambda b, i: (0, 0, 0))],
        out_specs=[pl.BlockSpec((None, ts, D), lambda b, i: (b, i, 0)),
                   pl.BlockSpec((None, E, ts), lambda b, i: (b, 0, i))],
        out_shape=[jax.ShapeDtypeStruct((B, S, D), F32), jax.ShapeDtypeStruct((B, E, S), F32)],
        compiler_params=_params("parallel", "parallel"),
        name="router",
    )(x1, g.reshape(1, D), mod3, wsplit)


def _cumsum_lanes(x):
    rows, S = x.shape
    tri = (lax.broadcasted_iota(jnp.int32, (LANES, LANES), 0)
           <= lax.broadcasted_iota(jnp.int32, (LANES, LANES), 1))
    tri = jnp.where(tri, 1.0, 0.0).astype(BF16)
    carry = jnp.zeros((rows, 1), F32)
    outs = []
    for c in range(S // LANES):
        inc = jnp.dot(x[:, c * LANES:(c + 1) * LANES].astype(BF16), tri, preferred_element_type=F32) + carry
        outs.append(inc)
        carry = inc[:, LANES - 1:LANES]
    return jnp.concatenate(outs, axis=1)


COMBINE_TM = 128
SLOT_CHUNK = 16
KBLOCK = 256


def _select_body(aff_ref, pos_ref, idx_ref, ts_ref, tc_ref, *, cap, tm):
    aff = aff_ref[...]
    E, S = aff.shape

    def count(mask):
        return jnp.sum(jnp.where(mask, 1.0, 0.0), axis=1, keepdims=True)

    def search(i, t):
        cand = t | jnp.left_shift(jnp.int32(1), 30 - i)
        return jnp.where(count(aff >= pltpu.bitcast(cand, F32)) >= cap, cand, t)

    thr = pltpu.bitcast(lax.fori_loop(0, 31, search, jnp.zeros((E, 1), jnp.int32)), F32)
    gt = aff > thr
    eq = aff == thr
    need = cap - count(gt)
    eq_rank = _cumsum_lanes(jnp.where(eq, 1.0, 0.0))
    sel = gt | (eq & (eq_rank <= need))
    rank = _cumsum_lanes(jnp.where(sel, 1.0, 0.0))
    sel = sel & (rank <= cap)
    sel_f = jnp.where(sel, 1.0, 0.0)
    pos = jnp.where(sel, rank - 1.0, -1.0).astype(jnp.int32)
    pos_ref[...] = pos

    tile_of = lax.broadcasted_iota(jnp.int32, (S, LANES), 0) // tm
    lane = lax.broadcasted_iota(jnp.int32, (S, LANES), 1)
    cnt = jnp.dot(sel_f.astype(BF16), jnp.where(tile_of == lane, 1.0, 0.0).astype(BF16),
                  preferred_element_type=F32)
    r = lax.broadcasted_iota(jnp.int32, (LANES, LANES), 0)
    c = lax.broadcasted_iota(jnp.int32, (LANES, LANES), 1)
    start = jnp.dot(cnt.astype(BF16), jnp.where(r < c, 1.0, 0.0).astype(BF16), preferred_element_type=F32)
    tc_ref[...] = cnt.astype(jnp.int32)
    ts_ref[...] = start.astype(jnp.int32)

    slot = lax.broadcasted_iota(jnp.int32, (cap, S), 0)
    tok = lax.broadcasted_iota(jnp.int32, (1, S), 1).astype(F32)
    for e in range(E):
        col = jnp.sum(jnp.where(pos[e:e + 1, :] == slot, tok, 0.0), axis=1, keepdims=True)
        idx_ref[e:e + 1, :] = jnp.broadcast_to(col, (cap, LANES)).T[0:1, :].astype(jnp.int32)


def select(aff_t, cap, tm):
    B, E, S = aff_t.shape
    assert S // tm <= LANES and tm <= 256
    blk = lambda n: pl.BlockSpec((None, E, n), lambda b: (b, 0, 0))
    return pl.pallas_call(
        functools.partial(_select_body, cap=cap, tm=tm),
        grid=(B,),
        in_specs=[blk(S)],
        out_specs=[blk(S), blk(cap), blk(LANES), blk(LANES)],
        out_shape=[jax.ShapeDtypeStruct((B, E, S), jnp.int32), jax.ShapeDtypeStruct((B, E, cap), jnp.int32),
                   jax.ShapeDtypeStruct((B, E, LANES), jnp.int32), jax.ShapeDtypeStruct((B, E, LANES), jnp.int32)],
        compiler_params=_params("parallel"),
        name="select",
    )(aff_t)


def _gather_rows_body(idx_ref, h_hbm, xe_ref, stage, sems, *, cap, n_exp):
    g = pl.program_id(0)
    slot = g % 2

    def row_copy(grp, slot, p, tok):
        return pltpu.make_async_copy(h_hbm.at[grp // n_exp, pl.ds(tok, 1), :], stage.at[slot, pl.ds(p, 1), :],
                                     sems.at[slot])

    def issue_group(grp, slot):
        def issue(p, carry):
            row_copy(grp, slot, p, idx_ref[grp * cap + p]).start()
            return carry

        lax.fori_loop(0, cap, issue, 0, unroll=8)

    @pl.when(g == 0)
    def _():
        issue_group(0, 0)

    @pl.when(g + 1 < pl.num_programs(0))
    def _():
        issue_group(g + 1, 1 - slot)

    def drain(p, carry):
        row_copy(g, slot, p, 0).wait()
        return carry

    lax.fori_loop(0, cap, drain, 0, unroll=8)
    xe_ref[...] = stage[slot].astype(xe_ref.dtype)


def gather_rows(idx, h2, cap):
    B, E, _ = idx.shape
    D = h2.shape[2]
    return pl.pallas_call(
        functools.partial(_gather_rows_body, cap=cap, n_exp=E),
        grid_spec=pltpu.PrefetchScalarGridSpec(
            num_scalar_prefetch=1,
            grid=(B * E,),
            in_specs=[pl.BlockSpec(memory_space=pl.ANY)],
            out_specs=pl.BlockSpec((None, cap, D), lambda g, idx: (g % E, g // E, 0)),
            scratch_shapes=[pltpu.VMEM((2, cap, D), F32), pltpu.SemaphoreType.DMA((2,))]),
        out_shape=jax.ShapeDtypeStruct((E, B * cap, D), BF16),
        compiler_params=_params("arbitrary"),
        name="gather_rows",
    )(idx.reshape(-1), h2)


def _up_body(x_ref, w1_ref, w3_ref, o_ref):
    x = x_ref[...]
    a = jnp.dot(x, w1_ref[...].astype(BF16), preferred_element_type=F32)
    b = jnp.dot(x, w3_ref[...].astype(BF16), preferred_element_type=F32)
    o_ref[...] = (_silu(a) * b).astype(o_ref.dtype)


def expert_up(xe, w1, w3):
    E, M, D = xe.shape
    Fd = w1.shape[2]
    tm = _pick(M, 1024, 8)
    tf = _pick(Fd, 256, 2 * LANES)
    return pl.pallas_call(
        _up_body,
        grid=(E, M // tm, Fd // tf),
        in_specs=[pl.BlockSpec((None, tm, D), lambda e, m, f: (e, m, 0)),
                  pl.BlockSpec((None, D, tf), lambda e, m, f: (e, 0, f)),
                  pl.BlockSpec((None, D, tf), lambda e, m, f: (e, 0, f))],
        out_specs=pl.BlockSpec((None, tm, tf), lambda e, m, f: (e, m, f)),
        out_shape=jax.ShapeDtypeStruct((E, M, Fd), BF16),
        compiler_params=_params("parallel", "parallel", "arbitrary"),
        name="expert_up",
    )(xe, w1, w3)


def _down_body(h_ref, w2_ref, o_ref):
    o_ref[...] = jnp.dot(h_ref[...], w2_ref[...].astype(BF16), preferred_element_type=F32).astype(o_ref.dtype)


def expert_down(hid, w2):
    E, M, Fd = hid.shape
    D = w2.shape[2]
    tm = _pick(M, 1024, 8)
    tn = _pick(D, 512, 2 * LANES)
    return pl.pallas_call(
        _down_body,
        grid=(E, M // tm, D // tn),
        in_specs=[pl.BlockSpec((None, tm, Fd), lambda e, m, n: (e, m, 0)),
                  pl.BlockSpec((None, Fd, tn), lambda e, m, n: (e, 0, n))],
        out_specs=pl.BlockSpec((None, tm, tn), lambda e, m, n: (e, m, n)),
        out_shape=jax.ShapeDtypeStruct((E, M, D), BF16),
        compiler_params=_params("parallel", "parallel", "arbitrary"),
        name="expert_down",
    )(hid, w2)


def _combine_body(ts_ref, tc_ref, pos_ref, aff_ref, ye_hbm, x_ref, mod_ref, g_ref, o_ref,
                  buf, wt, acc_sc, sem, *, cap, gate_row):
    b = pl.program_id(0)
    t = pl.program_id(1)
    E, tm = pos_ref.shape
    nt_pad = ts_ref.shape[0] // (pl.num_programs(0) * E)
    C = SLOT_CHUNK
    shift = C.bit_length() - 1

    @pl.when(jnp.logical_and(b == 0, t == 0))
    def _():
        buf[...] = jnp.zeros(buf.shape, buf.dtype)

    def chunk_copy(e, row0, k):
        return pltpu.make_async_copy(ye_hbm.at[e, pl.ds(row0, C), :], buf.at[pl.ds(k, C), :], sem)

    k = jnp.int32(0)
    for e in range(E):
        base = (b * E + e) * nt_pad + t
        first = ts_ref[base]
        n = tc_ref[base]
        first_al = lax.shift_left(lax.shift_right_logical(first, shift), shift)
        n_chunks = jnp.where(n > 0, lax.shift_right_logical(first + n - first_al + (C - 1), shift), 0)

        def chunk(c, k, e=e, first_al=first_al):
            p0 = first_al + c * C
            k = pl.multiple_of(k, C)
            chunk_copy(e, pl.multiple_of(b * cap + p0, C), k).start()
            slot = p0 + lax.broadcasted_iota(jnp.int32, (C, tm), 0)
            wt[pl.ds(k, C), :] = jnp.where(pos_ref[e:e + 1, :] == slot, aff_ref[e:e + 1, :], 0.0).astype(BF16)
            return k + C

        k = lax.fori_loop(0, n_chunks, chunk, k)

    n_used = lax.shift_right_logical(k, shift)
    n_blocks = lax.shift_right_logical(k + (KBLOCK - 1), KBLOCK.bit_length() - 1)

    def zero_tail(c, carry):
        wt[pl.ds(pl.multiple_of(c * C, C), C), :] = jnp.zeros((C, tm), BF16)
        return carry

    lax.fori_loop(n_used, n_blocks * (KBLOCK // C), zero_tail, 0)

    def drain(c, carry):
        chunk_copy(0, 0, 0).wait()
        return carry

    lax.fori_loop(0, n_used, drain, 0)
    acc_sc[...] = jnp.zeros(acc_sc.shape, F32)

    def block(kb, carry):
        r0 = pl.multiple_of(kb * KBLOCK, KBLOCK)
        acc_sc[...] += lax.dot_general(wt[pl.ds(r0, KBLOCK), :], buf[pl.ds(r0, KBLOCK), :], TN_DIMS,
                                       preferred_element_type=F32)
        return carry

    lax.fori_loop(0, n_blocks, block, 0)
    x2 = x_ref[...] + mod_ref[gate_row:gate_row + 1, :] * acc_sc[...]
    ms = jnp.mean(x2 * x2, axis=-1, keepdims=True)
    o_ref[...] = x2 * lax.rsqrt(ms + NORM_EPS) * g_ref[...]


def combine(tstart, tcount, pos, aff_t, ye, x1, mod3, final_g, cap, gate_row):
    B, E, S = pos.shape
    D = x1.shape[2]
    tm = min(COMBINE_TM, S)
    kcap = -(-E * (tm + SLOT_CHUNK) // KBLOCK) * KBLOCK
    return pl.pallas_call(
        functools.partial(_combine_body, cap=cap, gate_row=gate_row),
        grid_spec=pltpu.PrefetchScalarGridSpec(
            num_scalar_prefetch=2,
            grid=(B, S // tm),
            in_specs=[pl.BlockSpec((None, E, tm), lambda b, t, *_: (b, 0, t)),
                      pl.BlockSpec((None, E, tm), lambda b, t, *_: (b, 0, t)),
                      pl.BlockSpec(memory_space=pl.ANY),
                      pl.BlockSpec((None, tm, D), lambda b, t, *_: (b, t, 0)),
                      pl.BlockSpec((None, 6, D), lambda b, t, *_: (b, 0, 0)),
                      pl.BlockSpec((1, D), lambda b, t, *_: (0, 0))],
            out_specs=pl.BlockSpec((None, tm, D), lambda b, t, *_: (b, t, 0)),
            scratch_shapes=[pltpu.VMEM((kcap, D), BF16), pltpu.VMEM((kcap, tm), BF16),
                            pltpu.VMEM((tm, D), F32), pltpu.SemaphoreType.DMA(())]),
        out_shape=jax.ShapeDtypeStruct((B, S, D), F32),
        compiler_params=_params("arbitrary", "arbitrary"),
        name="combine",
    )(tstart.reshape(-1), tcount.reshape(-1), pos, aff_t, ye, x1, mod3, final_g.reshape(1, D))


def kernel(x, c, w_ada, b_ada, norm1_g, w_in, attn_q_norm_g, attn_k_norm_g, ret_decay_fwd, ret_decay_bwd,
           ret_norm_g, w_out, norm2_g, w_router, w1, w3, w2, final_g):
    B, S, D = x.shape
    T = B * S
    mix = w_out.shape[0]
    attn_w = mix // 2
    ret_w = mix - attn_w
    ah = attn_w // ATTN_HEAD_DIM
    kv_w = ATTN_KV_HEADS * ATTN_HEAD_DIM
    rh = ret_w // RET_HEAD_DIM
    E = w_router.shape[1]
    cap = CAPACITY_FACTOR * S // E

    o1 = attn_w + kv_w
    o2 = o1 + kv_w
    o3 = o2 + 2 * ret_w
    pb = PERM_BLOCK
    w_a = cast_cols(w_in, o1, pb, lambda j: j, _deinterleave_matrix(ATTN_HEAD_DIM), "cast_w_attn_qk")
    w_b = cast_cols(w_in, 2 * ret_w, pb, lambda j: o2 // pb + j, _deinterleave_matrix(RET_HEAD_DIM), "cast_w_ret_qk")
    w_c = cast_cols(w_in, kv_w + 2 * ret_w, kv_w, lambda j: jnp.where(j == 0, o1 // kv_w, o3 // kv_w + j - 1),
                    None, "cast_w_vg")
    w_o = cast_cols(w_out, D, _pick(D, 1024), lambda j: j, None, "cast_w_out")
    gq = _deinterleave_vec(attn_q_norm_g, ATTN_HEAD_DIM)
    gk = _deinterleave_vec(attn_k_norm_g, ATTN_HEAD_DIM)
    gain_a = jnp.concatenate([jnp.tile(gq, ah), jnp.tile(gk, ATTN_KV_HEADS)]).reshape(1, o1)
    q_scale = math.log2(math.e) / math.sqrt(ATTN_HEAD_DIM)
    cs_a = jnp.concatenate([jnp.full((attn_w,), q_scale, F32), jnp.ones((kv_w,), F32)]).reshape(1, o1)
    cs_b = jnp.concatenate([jnp.ones((ret_w,), F32), jnp.full((ret_w,), RET_HEAD_DIM ** -0.5, F32)]).reshape(1, 2 * ret_w)
    cos_a, sin_a = _rope_tables(S, ATTN_HEAD_DIM)
    cos_a2 = jnp.asarray(np.concatenate([cos_a, cos_a], axis=1))
    sin_a2 = jnp.asarray(np.concatenate([-sin_a, sin_a], axis=1))
    cos_r, sin_r = (jnp.asarray(t) for t in _rope_tables(S, RET_HEAD_DIM))

    mod3 = adaln(c, w_ada, b_ada).reshape(B, 6, D)

    h = modulate(x, norm1_g, mod3, 0, 1).reshape(T, D)
    qk_a = _proj_call(_proj_attn_body, h, w_a, [gain_a, cs_a], [cos_a2, sin_a2], S, 512, "proj_attn_qk")
    qk_r = _proj_call(_proj_ret_body, h, w_b, [cs_b], [cos_r, sin_r], S, 1024, "proj_ret_qk")
    vg = _proj_call(_proj_plain_body, h, w_c, [], [], S, 768, "proj_vg")
    attn_out = attention(qk_a.reshape(B, S, o1), vg.reshape(B, S, -1), ah)
    decays = jnp.stack([ret_decay_fwd, ret_decay_bwd]).astype(F32)
    v_off = kv_w // RET_HEAD_DIM
    ret_out = retention(qk_r.reshape(B, S, 2 * ret_w), vg.reshape(B, S, -1), decays, ret_norm_g.astype(F32),
                        rh, v_off, v_off + rh)
    x1 = outproj(attn_out.reshape(T, attn_w), ret_out.reshape(T, ret_w), w_o,
                 x.reshape(T, D), mod3, S, 2).reshape(B, S, D)

    h2, aff_t = router(x1, norm2_g, mod3, w_router, 3, 4)
    pos, idx, tstart, tcount = select(aff_t, cap, min(COMBINE_TM, S))
    xe = gather_rows(idx, h2, cap)
    ye = expert_down(expert_up(xe, w1, w3), w2)
    return combine(tstart, tcount, pos, aff_t, ye, x1, mod3, final_g, cap, 5)
```

```python
import functools
import math

import numpy as np
import jax
import jax.numpy as jnp
from jax import lax
from jax.experimental import pallas as pl
from jax.experimental.pallas import tpu as pltpu

F32 = jnp.float32
BF16 = jnp.bfloat16

GRID_W = 64
ROPE_THETA = 10000.0
NORM_EPS = 1e-6
ATTN_HEAD_DIM = 128
ATTN_KV_HEADS = 4
RET_HEAD_DIM = 256
RET_CHUNK = 128
CAPACITY_FACTOR = 2

LANES = 128
VMEM_LIMIT_BYTES = 56 * 1024 * 1024

NT_DIMS = (((1,), (1,)), ((), ()))
TN_DIMS = (((0,), (0,)), ((), ()))


def _params(*sem):
    return pltpu.CompilerParams(dimension_semantics=sem, vmem_limit_bytes=VMEM_LIMIT_BYTES)


def _pick(dim, pref, align=LANES):
    if dim <= pref:
        return dim
    t = (pref // align) * align
    while t >= align:
        if dim % t == 0:
            return t
        t -= align
    return dim


def _silu(v):
    return v * jax.nn.sigmoid(v)


def _adaln_body(c_ref, w_ref, b_ref, o_ref):
    c = c_ref[...]
    bp = c.shape[0]
    sc = _silu(c)
    hi = sc.astype(BF16).astype(F32)
    lhs = jnp.concatenate([hi, sc - hi], axis=0).astype(BF16)
    acc = jnp.dot(lhs, w_ref[...].astype(BF16), preferred_element_type=F32)
    o_ref[...] = acc[:bp] + acc[bp:] + b_ref[...]


def adaln(c, w_ada, b_ada):
    B, D = c.shape
    N = w_ada.shape[1]
    bp = -(-B // 8) * 8
    cp = jnp.pad(c, ((0, bp - B), (0, 0)))
    tn = _pick(N, 512)
    out = pl.pallas_call(
        _adaln_body,
        grid=(N // tn,),
        in_specs=[pl.BlockSpec((bp, D), lambda j: (0, 0)),
                  pl.BlockSpec((D, tn), lambda j: (0, j)),
                  pl.BlockSpec((1, tn), lambda j: (0, j))],
        out_specs=pl.BlockSpec((bp, tn), lambda j: (0, j)),
        out_shape=jax.ShapeDtypeStruct((bp, N), F32),
        compiler_params=_params("arbitrary"),
        name="adaln",
    )(cp, w_ada, b_ada.reshape(1, N))
    return out[:B]


def _modulated_norm(x, g, shift, scale):
    ms = jnp.mean(x * x, axis=-1, keepdims=True)
    return x * lax.rsqrt(ms + NORM_EPS) * g * (1.0 + scale) + shift


def _modulate_body(x_ref, g_ref, mod_ref, o_ref, *, shift_row, scale_row):
    h = _modulated_norm(x_ref[...], g_ref[...], mod_ref[shift_row:shift_row + 1, :],
                        mod_ref[scale_row:scale_row + 1, :])
    o_ref[...] = h.astype(o_ref.dtype)


def modulate(x, g, mod3, shift_row, scale_row):
    B, S, D = x.shape
    ts = _pick(S, 256, 8)
    return pl.pallas_call(
        functools.partial(_modulate_body, shift_row=shift_row, scale_row=scale_row),
        grid=(B, S // ts),
        in_specs=[pl.BlockSpec((None, ts, D), lambda b, i: (b, i, 0)),
                  pl.BlockSpec((1, D), lambda b, i: (0, 0)),
                  pl.BlockSpec((None, 6, D), lambda b, i: (b, 0, 0))],
        out_specs=pl.BlockSpec((None, ts, D), lambda b, i: (b, i, 0)),
        out_shape=jax.ShapeDtypeStruct((B, S, D), BF16),
        compiler_params=_params("parallel", "parallel"),
        name="modulate",
    )(x, g.reshape(1, D), mod3)


def _proj_attn_body(h_ref, w_ref, ones_ref, g_ref, cs_ref, cos_ref, sin_ref, o_ref):
    acc = jnp.dot(h_ref[...], w_ref[...], preferred_element_type=F32)
    d = ATTN_HEAD_DIM
    ssq = jnp.dot((acc * acc).astype(BF16), ones_ref[...], preferred_element_type=F32)
    y = acc * lax.rsqrt(ssq * (1.0 / d) + NORM_EPS) * g_ref[...]
    cosv = cos_ref[...]
    sinv = sin_ref[...]
    for hh in range(acc.shape[1] // d):
        sl = slice(hh * d, (hh + 1) * d)
        yh = y[:, sl]
        rot = pltpu.roll(yh, d // 2, axis=1)
        o_ref[:, sl] = ((yh * cosv + rot * sinv) * cs_ref[:, sl]).astype(o_ref.dtype)


def _proj_ret_body(h_ref, w_ref, cs_ref, cos_ref, sin_ref, o_ref):
    acc = jnp.dot(h_ref[...], w_ref[...], preferred_element_type=F32)
    cosv = cos_ref[...]
    sinv = sin_ref[...]
    d = RET_HEAD_DIM
    for hh in range(acc.shape[1] // d):
        s1 = slice(hh * d, hh * d + d // 2)
        s2 = slice(hh * d + d // 2, (hh + 1) * d)
        x1 = acc[:, s1]
        x2 = acc[:, s2]
        o_ref[:, s1] = ((x1 * cosv - x2 * sinv) * cs_ref[:, s1]).astype(o_ref.dtype)
        o_ref[:, s2] = ((x1 * sinv + x2 * cosv) * cs_ref[:, s2]).astype(o_ref.dtype)


def _proj_plain_body(h_ref, w_ref, o_ref):
    o_ref[...] = jnp.dot(h_ref[...], w_ref[...], preferred_element_type=F32).astype(o_ref.dtype)


def _proj_call(body, h2d, w, mats, rows, tables, S, tn, name):
    T, K = h2d.shape
    N = w.shape[1]
    tm = _pick(S, 1024, 8)
    per_b = S // tm
    in_specs = [pl.BlockSpec((tm, K), lambda i, j: (i, 0)),
                pl.BlockSpec((K, tn), lambda i, j: (0, j))]
    in_specs += [pl.BlockSpec((tn, tn), lambda i, j: (0, 0)) for _ in mats]
    in_specs += [pl.BlockSpec((1, tn), lambda i, j: (0, j)) for _ in rows]
    in_specs += [pl.BlockSpec((tm, t.shape[1]), lambda i, j: (i % per_b, 0)) for t in tables]
    return pl.pallas_call(
        body,
        grid=(T // tm, N // tn),
        in_specs=in_specs,
        out_specs=pl.BlockSpec((tm, tn), lambda i, j: (i, j)),
        out_shape=jax.ShapeDtypeStruct((T, N), BF16),
        compiler_params=_params("parallel", "arbitrary"),
        name=name,
    )(h2d, w, *mats, *rows, *tables)


def _rope_tables(S, d):
    quarter = d // 4
    t = np.arange(S)
    inv = ROPE_THETA ** (-np.arange(quarter, dtype=np.float64) / quarter)
    ang = np.concatenate([(t // GRID_W)[:, None] * inv, (t % GRID_W)[:, None] * inv], axis=-1)
    return np.cos(ang).astype(np.float32), np.sin(ang).astype(np.float32)


def _deinterleave_vec(g, d):
    return g.reshape(-1, d // 2, 2).transpose(0, 2, 1).reshape(-1)


PERM_BLOCK = 256


def _deinterleave_matrix(d):
    src = np.concatenate([h * d + np.concatenate([np.arange(0, d, 2), np.arange(1, d, 2)])
                          for h in range(PERM_BLOCK // d)])
    p = np.zeros((PERM_BLOCK, PERM_BLOCK), np.float32)
    p[src, np.arange(PERM_BLOCK)] = 1.0
    return jnp.asarray(p, BF16)


def _cast_perm_body(w_ref, p_ref, o_ref):
    o_ref[...] = jnp.dot(w_ref[...].astype(BF16), p_ref[...], preferred_element_type=F32).astype(o_ref.dtype)


def _cast_body(w_ref, o_ref):
    o_ref[...] = w_ref[...].astype(o_ref.dtype)


def cast_cols(w, n_out, bw, in_block, perm, name):
    K = w.shape[0]
    tr = _pick(K, 2048, 8)
    in_specs = [pl.BlockSpec((tr, bw), lambda i, j: (i, in_block(j)))]
    args = [w]
    if perm is not None:
        in_specs.append(pl.BlockSpec((bw, bw), lambda i, j: (0, 0)))
        args.append(perm)
    return pl.pallas_call(
        _cast_body if perm is None else _cast_perm_body,
        grid=(K // tr, n_out // bw),
        in_specs=in_specs,
        out_specs=pl.BlockSpec((tr, bw), lambda i, j: (i, j)),
        out_shape=jax.ShapeDtypeStruct((K, n_out), BF16),
        compiler_params=_params("parallel", "parallel"),
        name=name,
    )(*args)


ATTN_TQ = 256
ATTN_TK = 512
ONES_ROWS = 16
ATTN_SAFE_EXP = 50.0


def _attn_body(q_ref, k_ref, v_ref, o_ref, vt_sc, qt_sc, kmax_sc, m_sc, acc_sc, *, tk, groups):
    tq = q_ref.shape[0]
    S, dh = k_ref.shape
    nk = S // tk

    @pl.when(pl.program_id(2) == 0)
    def _():
        for j in range(nk):
            vt_sc[j, :dh, :] = v_ref[j * tk:(j + 1) * tk, :].astype(F32).T.astype(BF16)
            vt_sc[j, dh:, :] = jnp.ones((ONES_ROWS, tk), BF16)
        kf = k_ref[...].astype(F32)
        ksq = jnp.max(jnp.sum(kf * kf, axis=1, keepdims=True), axis=0, keepdims=True)
        kmax_sc[...] = jnp.broadcast_to(jnp.sqrt(ksq), kmax_sc.shape)

    for g in range(groups):
        qt = q_ref[:, g * dh:(g + 1) * dh].astype(F32).T
        qt_sc[:, g * tq:(g + 1) * tq] = qt.astype(BF16)
        m_sc[:, g * tq:(g + 1) * tq] = jnp.sqrt(jnp.sum(qt * qt, axis=0, keepdims=True)) * kmax_sc[:, 0:1]
    acc_sc[...] = jnp.zeros(acc_sc.shape, F32)
    bounded = jnp.max(m_sc[...]) <= ATTN_SAFE_EXP

    @pl.when(bounded)
    def _():
        def step(j, carry):
            off = pl.multiple_of(j * tk, tk)
            st = jnp.dot(k_ref[pl.ds(off, tk), :], qt_sc[...], preferred_element_type=F32)
            p = jnp.exp2(st - m_sc[...]).astype(BF16)
            acc_sc[...] += jnp.dot(vt_sc[j], p, preferred_element_type=F32)
            return carry

        lax.fori_loop(0, nk, step, 0, unroll=True)

    @pl.when(jnp.logical_not(bounded))
    def _():
        m_sc[...] = jnp.full(m_sc.shape, -jnp.inf, F32)

        def step(j, carry):
            off = pl.multiple_of(j * tk, tk)
            st = jnp.dot(k_ref[pl.ds(off, tk), :], qt_sc[...], preferred_element_type=F32)
            m_prev = m_sc[...]
            m_new = jnp.maximum(m_prev, jnp.max(st, axis=0, keepdims=True))
            p = jnp.exp2(st - m_new).astype(BF16)
            acc_sc[...] = acc_sc[...] * jnp.exp2(m_prev - m_new) + jnp.dot(vt_sc[j], p, preferred_element_type=F32)
            m_sc[...] = m_new
            return carry

        lax.fori_loop(0, nk, step, 0)

    ot = acc_sc[:dh, :] / acc_sc[dh:dh + 1, :]
    for g in range(groups):
        o_ref[:, g * dh:(g + 1) * dh] = ot[:, g * tq:(g + 1) * tq].T.astype(o_ref.dtype)


def attention(qk, vg, n_heads):
    B, S, _ = qk.shape
    dh = ATTN_HEAD_DIM
    kvh = ATTN_KV_HEADS
    groups = n_heads // kvh
    tq = _pick(S, ATTN_TQ, 8)
    tk = _pick(S, ATTN_TK, 8)
    gw = groups * dh
    return pl.pallas_call(
        functools.partial(_attn_body, tk=tk, groups=groups),
        grid=(B, kvh, S // tq),
        in_specs=[pl.BlockSpec((None, tq, gw), lambda b, k, i: (b, i, k)),
                  pl.BlockSpec((None, S, dh), lambda b, k, i: (b, 0, n_heads + k)),
                  pl.BlockSpec((None, S, dh), lambda b, k, i: (b, 0, k))],
        out_specs=pl.BlockSpec((None, tq, gw), lambda b, k, i: (b, i, k)),
        out_shape=jax.ShapeDtypeStruct((B, S, n_heads * dh), BF16),
        scratch_shapes=[pltpu.VMEM((S // tk, dh + ONES_ROWS, tk), BF16),
                        pltpu.VMEM((dh, groups * tq), BF16),
                        pltpu.VMEM((1, LANES), F32),
                        pltpu.VMEM((1, groups * tq), F32),
                        pltpu.VMEM((dh + ONES_ROWS, groups * tq), F32)],
        compiler_params=_params("arbitrary", "arbitrary", "arbitrary"),
        name="attention",
    )(qk, qk, vg)


RET_UNROLL = 8


def _ret_body(dec_ref, q_ref, k_ref, v_ref, gr_ref, g_ref, o_ref, rf_sc, rb_sc, racc_sc, *, chunk):
    C = chunk
    S, dk = q_ref.shape
    n = S // C
    h = pl.program_id(1)
    lgf = -jnp.exp(jnp.full((1, 1), dec_ref[0, h], F32))
    lgb = -jnp.exp(jnp.full((1, 1), dec_ref[1, h], F32))
    idx = lax.broadcasted_iota(jnp.int32, (C, 1), 0).astype(F32)
    zeta_f = jnp.exp(lgf * (C - 1.0 - idx))
    xi_f = jnp.exp(lgf * (idx + 1.0))
    zeta_b = jnp.exp(lgb * idx)
    xi_b = jnp.exp(lgb * (C - idx))
    cd_f = jnp.exp(lgf * C)
    cd_b = jnp.exp(lgb * C)
    diff = (lax.broadcasted_iota(jnp.int32, (C, C), 0) - lax.broadcasted_iota(jnp.int32, (C, C), 1)).astype(F32)
    decay = jnp.where(diff >= 0, jnp.exp(lgf * jnp.maximum(diff, 0.0)), jnp.exp(lgb * jnp.maximum(-diff, 0.0)))

    def kv_update(i, zeta, cd):
        off = pl.multiple_of(i * C, C)
        kz = (k_ref[pl.ds(off, C), :].astype(F32) * zeta).astype(BF16)
        kv = lax.dot_general(kz, v_ref[pl.ds(off, C), :], TN_DIMS, preferred_element_type=F32)
        racc_sc[...] = racc_sc[...] * cd + kv

    racc_sc[...] = jnp.zeros(racc_sc.shape, F32)

    def fstep(i, carry):
        rf_sc[i] = racc_sc[...].astype(BF16)
        kv_update(i, zeta_f, cd_f)
        return carry

    lax.fori_loop(0, n, fstep, 0, unroll=RET_UNROLL)
    racc_sc[...] = jnp.zeros(racc_sc.shape, F32)

    def bstep(t, carry):
        i = n - 1 - t
        rb_sc[i] = racc_sc[...].astype(BF16)
        kv_update(i, zeta_b, cd_b)
        return carry

    lax.fori_loop(0, n, bstep, 0, unroll=RET_UNROLL)
    gain = g_ref[...]

    def ostep(i, carry):
        off = pl.multiple_of(i * C, C)
        qi = q_ref[pl.ds(off, C), :]
        ki = k_ref[pl.ds(off, C), :]
        vi = v_ref[pl.ds(off, C), :]
        s = lax.dot_general(qi, ki, NT_DIMS, preferred_element_type=F32) * decay
        qf = (qi.astype(F32) * xi_f).astype(BF16)
        qb = (qi.astype(F32) * xi_b).astype(BF16)
        o = (jnp.dot(s.astype(BF16), vi, preferred_element_type=F32)
             + jnp.dot(qf, rf_sc[i], preferred_element_type=F32)
             + jnp.dot(qb, rb_sc[i], preferred_element_type=F32))
        mu = jnp.mean(o, axis=-1, keepdims=True)
        d = o - mu
        var = jnp.mean(d * d, axis=-1, keepdims=True)
        y = d * lax.rsqrt(var + NORM_EPS) * gain
        o_ref[pl.ds(off, C), :] = (y * _silu(gr_ref[pl.ds(off, C), :].astype(F32))).astype(o_ref.dtype)
        return carry

    lax.fori_loop(0, n, ostep, 0, unroll=RET_UNROLL)


def retention(qk, vg, decays, gain, n_heads, v_off, g_off):
    B, S, _ = qk.shape
    d = RET_HEAD_DIM
    n = S // RET_CHUNK
    blk = lambda off: pl.BlockSpec((None, S, d), lambda b, h: (b, 0, off + h))
    return pl.pallas_call(
        functools.partial(_ret_body, chunk=RET_CHUNK),
        grid=(B, n_heads),
        in_specs=[pl.BlockSpec(memory_space=pltpu.SMEM),
                  blk(0), blk(n_heads), blk(v_off), blk(g_off),
                  pl.BlockSpec((None, 1, d), lambda b, h: (h, 0, 0))],
        out_specs=blk(0),
        out_shape=jax.ShapeDtypeStruct((B, S, n_heads * d), BF16),
        scratch_shapes=[pltpu.VMEM((n, d, d), BF16), pltpu.VMEM((n, d, d), BF16), pltpu.VMEM((d, d), F32)],
        compiler_params=_params("parallel", "parallel"),
        name="retention",
    )(decays, qk, qk, vg, vg, gain.reshape(n_heads, 1, d))


def _outproj_body(a_ref, r_ref, wa_ref, wr_ref, x_ref, mod_ref, o_ref, *, gate_row):
    acc = (jnp.dot(a_ref[...], wa_ref[...], preferred_element_type=F32)
           + jnp.dot(r_ref[...], wr_ref[...], preferred_element_type=F32))
    o_ref[...] = x_ref[...] + mod_ref[gate_row:gate_row + 1, :] * acc


def outproj(attn_out, ret_out, w_o, x2d, mod3, S, gate_row):
    T, Ka = attn_out.shape
    Kr = ret_out.shape[1]
    assert Ka == Kr and w_o.shape[0] == Ka + Kr
    N = w_o.shape[1]
    tm = _pick(S, 1024, 8)
    tn = _pick(N, 512, 2 * LANES)
    per_b = S // tm
    return pl.pallas_call(
        functools.partial(_outproj_body, gate_row=gate_row),
        grid=(T // tm, N // tn),
        in_specs=[pl.BlockSpec((tm, Ka), lambda i, j: (i, 0)),
                  pl.BlockSpec((tm, Kr), lambda i, j: (i, 0)),
                  pl.BlockSpec((Ka, tn), lambda i, j: (0, j)),
                  pl.BlockSpec((Kr, tn), lambda i, j: (1, j)),
                  pl.BlockSpec((tm, tn), lambda i, j: (i, j)),
                  pl.BlockSpec((None, 6, tn), lambda i, j: (i // per_b, 0, j))],
        out_specs=pl.BlockSpec((tm, tn), lambda i, j: (i, j)),
        out_shape=jax.ShapeDtypeStruct((T, N), F32),
        compiler_params=_params("parallel", "arbitrary"),
        name="outproj",
    )(attn_out, ret_out, w_o, w_o, x2d, mod3)


def _router_body(x_ref, g_ref, mod_ref, wr_ref, h_ref, aff_ref, *, shift_row, scale_row):
    h = _modulated_norm(x_ref[...], g_ref[...], mod_ref[shift_row:shift_row + 1, :],
                        mod_ref[scale_row:scale_row + 1, :])
    h_ref[...] = h
    hi = h.astype(BF16)
    lo = (h - hi.astype(F32)).astype(BF16)
    whi = wr_ref[0]
    wlo = wr_ref[1]
    lg = (lax.dot_general(whi, hi, NT_DIMS, preferred_element_type=F32)
          + lax.dot_general(whi, lo, NT_DIMS, preferred_element_type=F32)
          + lax.dot_general(wlo, hi, NT_DIMS, preferred_element_type=F32))
    e = jnp.exp(lg - jnp.max(lg, axis=0, keepdims=True))
    aff_ref[...] = e / jnp.sum(e, axis=0, keepdims=True)


def router(x1, g, mod3, w_router, shift_row, scale_row):
    B, S, D = x1.shape
    E = w_router.shape[1]
    wt = w_router.T
    whi = wt.astype(BF16)
    wsplit = jnp.stack([whi, (wt - whi.astype(F32)).astype(BF16)])
    ts = _pick(S, 256)
    return pl.pallas_call(
        functools.partial(_router_body, shift_row=shift_row, scale_row=scale_row),
        grid=(B, S // ts),
        in_specs=[pl.BlockSpec((None, ts, D), lambda b, i: (b, i, 0)),
                  pl.BlockSpec((1, D), lambda b, i: (0, 0)),
                  pl.BlockSpec((None, 6, D), lambda b, i: (b, 0, 0)),
                  pl.BlockSpec((2, E, D), lambda b, i: (0, 0, 0))],
        out_specs=[pl.BlockSpec((None, ts, D), lambda b, i: (b, i, 0)),
                   pl.BlockSpec((None, E, ts), lambda b, i: (b, 0, i))],
        out_shape=[jax.ShapeDtypeStruct((B, S, D), F32), jax.ShapeDtypeStruct((B, E, S), F32)],
        compiler_params=_params("parallel", "parallel"),
        name="router",
    )(x1, g.reshape(1, D), mod3, wsplit)


def _cumsum_lanes(x):
    rows, S = x.shape
    tri = (lax.broadcasted_iota(jnp.int32, (LANES, LANES), 0)
           <= lax.broadcasted_iota(jnp.int32, (LANES, LANES), 1))
    tri = jnp.where(tri, 1.0, 0.0).astype(BF16)
    carry = jnp.zeros((rows, 1), F32)
    outs = []
    for c in range(S // LANES):
        inc = jnp.dot(x[:, c * LANES:(c + 1) * LANES].astype(BF16), tri, preferred_element_type=F32) + carry
        outs.append(inc)
        carry = inc[:, LANES - 1:LANES]
    return jnp.concatenate(outs, axis=1)


COMBINE_TM = 128
SLOT_CHUNK = 16
KBLOCK = 256


def _select_body(aff_ref, pos_ref, idx_ref, ts_ref, tc_ref, *, cap, tm):
    aff = aff_ref[...]
    E, S = aff.shape

    def count(mask):
        return jnp.sum(jnp.where(mask, 1.0, 0.0), axis=1, keepdims=True)

    def search(i, t):
        cand = t | jnp.left_shift(jnp.int32(1), 30 - i)
        return jnp.where(count(aff >= pltpu.bitcast(cand, F32)) >= cap, cand, t)

    thr = pltpu.bitcast(lax.fori_loop(0, 31, search, jnp.zeros((E, 1), jnp.int32)), F32)
    gt = aff > thr
    eq = aff == thr
    need = cap - count(gt)
    eq_rank = _cumsum_lanes(jnp.where(eq, 1.0, 0.0))
    sel = gt | (eq & (eq_rank <= need))
    rank = _cumsum_lanes(jnp.where(sel, 1.0, 0.0))
    sel = sel & (rank <= cap)
    sel_f = jnp.where(sel, 1.0, 0.0)
    pos = jnp.where(sel, rank - 1.0, -1.0).astype(jnp.int32)
    pos_ref[...] = pos

    tile_of = lax.broadcasted_iota(jnp.int32, (S, LANES), 0) // tm
    lane = lax.broadcasted_iota(jnp.int32, (S, LANES), 1)
    cnt = jnp.dot(sel_f.astype(BF16), jnp.where(tile_of == lane, 1.0, 0.0).astype(BF16),
                  preferred_element_type=F32)
    r = lax.broadcasted_iota(jnp.int32, (LANES, LANES), 0)
    c = lax.broadcasted_iota(jnp.int32, (LANES, LANES), 1)
    start = jnp.dot(cnt.astype(BF16), jnp.where(r < c, 1.0, 0.0).astype(BF16), preferred_element_type=F32)
    tc_ref[...] = cnt.astype(jnp.int32)
    ts_ref[...] = start.astype(jnp.int32)

    slot = lax.broadcasted_iota(jnp.int32, (cap, S), 0)
    tok = lax.broadcasted_iota(jnp.int32, (1, S), 1).astype(F32)
    for e in range(E):
        col = jnp.sum(jnp.where(pos[e:e + 1, :] == slot, tok, 0.0), axis=1, keepdims=True)
        idx_ref[e:e + 1, :] = jnp.broadcast_to(col, (cap, LANES)).T[0:1, :].astype(jnp.int32)


def select(aff_t, cap, tm):
    B, E, S = aff_t.shape
    assert S // tm <= LANES and tm <= 256
    blk = lambda n: pl.BlockSpec((None, E, n), lambda b: (b, 0, 0))
    return pl.pallas_call(
        functools.partial(_select_body, cap=cap, tm=tm),
        grid=(B,),
        in_specs=[blk(S)],
        out_specs=[blk(S), blk(cap), blk(LANES), blk(LANES)],
        out_shape=[jax.ShapeDtypeStruct((B, E, S), jnp.int32), jax.ShapeDtypeStruct((B, E, cap), jnp.int32),
                   jax.ShapeDtypeStruct((B, E, LANES), jnp.int32), jax.ShapeDtypeStruct((B, E, LANES), jnp.int32)],
        compiler_params=_params("parallel"),
        name="select",
    )(aff_t)


def _gather_rows_body(idx_ref, h_hbm, xe_ref, stage, sems, *, cap, n_exp):
    g = pl.program_id(0)
    slot = g % 2

    def row_copy(grp, slot, p, tok):
        return pltpu.make_async_copy(h_hbm.at[grp // n_exp, pl.ds(tok, 1), :], stage.at[slot, pl.ds(p, 1), :],
                                     sems.at[slot])

    def issue_group(grp, slot):
        def issue(p, carry):
            row_copy(grp, slot, p, idx_ref[grp * cap + p]).start()
            return carry

        lax.fori_loop(0, cap, issue, 0, unroll=8)

    @pl.when(g == 0)
    def _():
        issue_group(0, 0)

    @pl.when(g + 1 < pl.num_programs(0))
    def _():
        issue_group(g + 1, 1 - slot)

    def drain(p, carry):
        row_copy(g, slot, p, 0).wait()
        return carry

    lax.fori_loop(0, cap, drain, 0, unroll=8)
    xe_ref[...] = stage[slot].astype(xe_ref.dtype)


def gather_rows(idx, h2, cap):
    B, E, _ = idx.shape
    D = h2.shape[2]
    return pl.pallas_call(
        functools.partial(_gather_rows_body, cap=cap, n_exp=E),
        grid_spec=pltpu.PrefetchScalarGridSpec(
            num_scalar_prefetch=1,
            grid=(B * E,),
            in_specs=[pl.BlockSpec(memory_space=pl.ANY)],
            out_specs=pl.BlockSpec((None, cap, D), lambda g, idx: (g % E, g // E, 0)),
            scratch_shapes=[pltpu.VMEM((2, cap, D), F32), pltpu.SemaphoreType.DMA((2,))]),
        out_shape=jax.ShapeDtypeStruct((E, B * cap, D), BF16),
        compiler_params=_params("arbitrary"),
        name="gather_rows",
    )(idx.reshape(-1), h2)


def _up_body(x_ref, w1_ref, w3_ref, o_ref):
    x = x_ref[...]
    a = jnp.dot(x, w1_ref[...].astype(BF16), preferred_element_type=F32)
    b = jnp.dot(x, w3_ref[...].astype(BF16), preferred_element_type=F32)
    o_ref[...] = (_silu(a) * b).astype(o_ref.dtype)


def expert_up(xe, w1, w3):
    E, M, D = xe.shape
    Fd = w1.shape[2]
    tm = _pick(M, 1024, 8)
    tf = _pick(Fd, 256, 2 * LANES)
    return pl.pallas_call(
        _up_body,
        grid=(E, M // tm, Fd // tf),
        in_specs=[pl.BlockSpec((None, tm, D), lambda e, m, f: (e, m, 0)),
                  pl.BlockSpec((None, D, tf), lambda e, m, f: (e, 0, f)),
                  pl.BlockSpec((None, D, tf), lambda e, m, f: (e, 0, f))],
        out_specs=pl.BlockSpec((None, tm, tf), lambda e, m, f: (e, m, f)),
        out_shape=jax.ShapeDtypeStruct((E, M, Fd), BF16),
        compiler_params=_params("parallel", "parallel", "arbitrary"),
        name="expert_up",
    )(xe, w1, w3)


def _down_body(h_ref, w2_ref, o_ref):
    o_ref[...] = jnp.dot(h_ref[...], w2_ref[...].astype(BF16), preferred_element_type=F32).astype(o_ref.dtype)


def expert_down(hid, w2):
    E, M, Fd = hid.shape
    D = w2.shape[2]
    tm = _pick(M, 2048, 8)
    tn = _pick(D, 512, 2 * LANES)
    return pl.pallas_call(
        _down_body,
        grid=(E, M // tm, D // tn),
        in_specs=[pl.BlockSpec((None, tm, Fd), lambda e, m, n: (e, m, 0)),
                  pl.BlockSpec((None, Fd, tn), lambda e, m, n: (e, 0, n))],
        out_specs=pl.BlockSpec((None, tm, tn), lambda e, m, n: (e, m, n)),
        out_shape=jax.ShapeDtypeStruct((E, M, D), BF16),
        compiler_params=_params("parallel", "parallel", "arbitrary"),
        name="expert_down",
    )(hid, w2)


def _combine_body(ts_ref, tc_ref, pos_ref, aff_ref, ye_hbm, x_ref, mod_ref, g_ref, o_ref,
                  buf, wt, acc_sc, sems, *, cap, gate_row):
    b = pl.program_id(0)
    t = pl.program_id(1)
    n_b = pl.num_programs(0)
    n_t = pl.num_programs(1)
    E, tm = pos_ref.shape
    nt_pad = ts_ref.shape[0] // (n_b * E)
    C = SLOT_CHUNK
    shift = C.bit_length() - 1
    step = b * n_t + t
    slot = step % 2

    def chunk_copy(e, row0, slot_, k):
        return pltpu.make_async_copy(ye_hbm.at[e, pl.ds(row0, C), :], buf.at[slot_, pl.ds(k, C), :], sems.at[slot_])

    def for_each_chunk(bb, tt, fn):
        k = jnp.int32(0)
        for e in range(E):
            base = (bb * E + e) * nt_pad + tt
            first = ts_ref[base]
            n = tc_ref[base]
            first_al = lax.shift_left(lax.shift_right_logical(first, shift), shift)
            n_chunks = jnp.where(n > 0, lax.shift_right_logical(first + n - first_al + (C - 1), shift), 0)

            def chunk(c, k, e=e, first_al=first_al):
                k = pl.multiple_of(k, C)
                fn(e, first_al + c * C, k)
                return k + C

            k = lax.fori_loop(0, n_chunks, chunk, k)
        return k

    def start_tile(bb, tt, slot_):
        for_each_chunk(bb, tt, lambda e, p0, k: chunk_copy(e, pl.multiple_of(bb * cap + p0, C), slot_, k).start())

    @pl.when(step == 0)
    def _():
        buf[...] = jnp.zeros(buf.shape, buf.dtype)
        start_tile(0, 0, 0)

    @pl.when(step + 1 < n_b * n_t)
    def _():
        wrap = t + 1 == n_t
        start_tile(jnp.where(wrap, b + 1, b), jnp.where(wrap, 0, t + 1), 1 - slot)

    def weights(e, p0, k):
        rows = p0 + lax.broadcasted_iota(jnp.int32, (C, tm), 0)
        wt[pl.ds(k, C), :] = jnp.where(pos_ref[e:e + 1, :] == rows, aff_ref[e:e + 1, :], 0.0).astype(BF16)

    k = for_each_chunk(b, t, weights)
    n_used = lax.shift_right_logical(k, shift)
    n_blocks = lax.shift_right_logical(k + (KBLOCK - 1), KBLOCK.bit_length() - 1)

    def zero_tail(c, carry):
        wt[pl.ds(pl.multiple_of(c * C, C), C), :] = jnp.zeros((C, tm), BF16)
        return carry

    lax.fori_loop(n_used, n_blocks * (KBLOCK // C), zero_tail, 0)

    def drain(c, carry):
        chunk_copy(0, 0, slot, 0).wait()
        return carry

    lax.fori_loop(0, n_used, drain, 0)
    acc_sc[...] = jnp.zeros(acc_sc.shape, F32)

    def block(kb, carry):
        r0 = pl.multiple_of(kb * KBLOCK, KBLOCK)
        acc_sc[...] += lax.dot_general(wt[pl.ds(r0, KBLOCK), :], buf[slot, pl.ds(r0, KBLOCK), :], TN_DIMS,
                                       preferred_element_type=F32)
        return carry

    lax.fori_loop(0, n_blocks, block, 0)
    x2 = x_ref[...] + mod_ref[gate_row:gate_row + 1, :] * acc_sc[...]
    ms = jnp.mean(x2 * x2, axis=-1, keepdims=True)
    o_ref[...] = x2 * lax.rsqrt(ms + NORM_EPS) * g_ref[...]


def combine(tstart, tcount, pos, aff_t, ye, x1, mod3, final_g, cap, gate_row):
    B, E, S = pos.shape
    D = x1.shape[2]
    tm = min(COMBINE_TM, S)
    kcap = -(-E * (tm + SLOT_CHUNK) // KBLOCK) * KBLOCK
    return pl.pallas_call(
        functools.partial(_combine_body, cap=cap, gate_row=gate_row),
        grid_spec=pltpu.PrefetchScalarGridSpec(
            num_scalar_prefetch=2,
            grid=(B, S // tm),
            in_specs=[pl.BlockSpec((None, E, tm), lambda b, t, *_: (b, 0, t)),
                      pl.BlockSpec((None, E, tm), lambda b, t, *_: (b, 0, t)),
                      pl.BlockSpec(memory_space=pl.ANY),
                      pl.BlockSpec((None, tm, D), lambda b, t, *_: (b, t, 0)),
                      pl.BlockSpec((None, 6, D), lambda b, t, *_: (b, 0, 0)),
                      pl.BlockSpec((1, D), lambda b, t, *_: (0, 0))],
            out_specs=pl.BlockSpec((None, tm, D), lambda b, t, *_: (b, t, 0)),
            scratch_shapes=[pltpu.VMEM((2, kcap, D), BF16), pltpu.VMEM((kcap, tm), BF16),
                            pltpu.VMEM((tm, D), F32), pltpu.SemaphoreType.DMA((2,))]),
        out_shape=jax.ShapeDtypeStruct((B, S, D), F32),
        compiler_params=_params("arbitrary", "arbitrary"),
        name="combine",
    )(tstart.reshape(-1), tcount.reshape(-1), pos, aff_t, ye, x1, mod3, final_g.reshape(1, D))


def kernel(x, c, w_ada, b_ada, norm1_g, w_in, attn_q_norm_g, attn_k_norm_g, ret_decay_fwd, ret_decay_bwd,
           ret_norm_g, w_out, norm2_g, w_router, w1, w3, w2, final_g):
    B, S, D = x.shape
    T = B * S
    mix = w_out.shape[0]
    attn_w = mix // 2
    ret_w = mix - attn_w
    ah = attn_w // ATTN_HEAD_DIM
    kv_w = ATTN_KV_HEADS * ATTN_HEAD_DIM
    rh = ret_w // RET_HEAD_DIM
    E = w_router.shape[1]
    cap = CAPACITY_FACTOR * S // E

    o1 = attn_w + kv_w
    o2 = o1 + kv_w
    o3 = o2 + 2 * ret_w
    pb = PERM_BLOCK
    w_a = cast_cols(w_in, o1, pb, lambda j: j, _deinterleave_matrix(ATTN_HEAD_DIM), "cast_w_attn_qk")
    w_b = cast_cols(w_in, 2 * ret_w, pb, lambda j: o2 // pb + j, _deinterleave_matrix(RET_HEAD_DIM), "cast_w_ret_qk")
    w_c = cast_cols(w_in, kv_w + 2 * ret_w, kv_w, lambda j: jnp.where(j == 0, o1 // kv_w, o3 // kv_w + j - 1),
                    None, "cast_w_vg")
    w_o = cast_cols(w_out, D, _pick(D, 1024), lambda j: j, None, "cast_w_out")
    gq = _deinterleave_vec(attn_q_norm_g, ATTN_HEAD_DIM)
    gk = _deinterleave_vec(attn_k_norm_g, ATTN_HEAD_DIM)
    gain_a = jnp.concatenate([jnp.tile(gq, ah), jnp.tile(gk, ATTN_KV_HEADS)]).reshape(1, o1)
    q_scale = math.log2(math.e) / math.sqrt(ATTN_HEAD_DIM)
    cs_a = jnp.concatenate([jnp.full((attn_w,), q_scale, F32), jnp.ones((kv_w,), F32)]).reshape(1, o1)
    cs_b = jnp.concatenate([jnp.ones((ret_w,), F32), jnp.full((ret_w,), RET_HEAD_DIM ** -0.5, F32)]).reshape(1, 2 * ret_w)
    cos_a, sin_a = _rope_tables(S, ATTN_HEAD_DIM)
    cos_a2 = jnp.asarray(np.concatenate([cos_a, cos_a], axis=1))
    sin_a2 = jnp.asarray(np.concatenate([-sin_a, sin_a], axis=1))
    cos_r, sin_r = (jnp.asarray(t) for t in _rope_tables(S, RET_HEAD_DIM))

    mod3 = adaln(c, w_ada, b_ada).reshape(B, 6, D)

    h = modulate(x, norm1_g, mod3, 0, 1).reshape(T, D)
    tn_a = _pick(o1, 512, 2 * LANES)
    head_of = np.arange(tn_a) // ATTN_HEAD_DIM
    head_ones = jnp.asarray(head_of[:, None] == head_of[None, :], BF16)
    qk_a = _proj_call(_proj_attn_body, h, w_a, [head_ones], [gain_a, cs_a], [cos_a2, sin_a2], S, tn_a, "proj_attn_qk")
    qk_r = _proj_call(_proj_ret_body, h, w_b, [], [cs_b], [cos_r, sin_r], S, _pick(2 * ret_w, 1024, 2 * LANES),
                      "proj_ret_qk")
    vg = _proj_call(_proj_plain_body, h, w_c, [], [], [], S, _pick(kv_w + 2 * ret_w, 768, 2 * LANES), "proj_vg")
    attn_out = attention(qk_a.reshape(B, S, o1), vg.reshape(B, S, -1), ah)
    decays = jnp.stack([ret_decay_fwd, ret_decay_bwd]).astype(F32)
    v_off = kv_w // RET_HEAD_DIM
    ret_out = retention(qk_r.reshape(B, S, 2 * ret_w), vg.reshape(B, S, -1), decays, ret_norm_g.astype(F32),
                        rh, v_off, v_off + rh)
    x1 = outproj(attn_out.reshape(T, attn_w), ret_out.reshape(T, ret_w), w_o,
                 x.reshape(T, D), mod3, S, 2).reshape(B, S, D)

    h2, aff_t = router(x1, norm2_g, mod3, w_router, 3, 4)
    pos, idx, tstart, tcount = select(aff_t, cap, min(COMBINE_TM, S))
    xe = gather_rows(idx, h2, cap)
    ye = expert_down(expert_up(xe, w1, w3), w2)
    return combine(tstart, tcount, pos, aff_t, ye, x1, mod3, final_g, cap, 5)
```

```python
import functools
import math

import numpy as np
import jax
import jax.numpy as jnp
from jax import lax
from jax.experimental import pallas as pl
from jax.experimental.pallas import tpu as pltpu

F32 = jnp.float32
BF16 = jnp.bfloat16

GRID_W = 64
ROPE_THETA = 10000.0
NORM_EPS = 1e-6
ATTN_HEAD_DIM = 128
ATTN_KV_HEADS = 4
RET_HEAD_DIM = 256
RET_CHUNK = 128
CAPACITY_FACTOR = 2

LANES = 128
VMEM_LIMIT_BYTES = 56 * 1024 * 1024

NT_DIMS = (((1,), (1,)), ((), ()))
TN_DIMS = (((0,), (0,)), ((), ()))


def _params(*sem):
    return pltpu.CompilerParams(dimension_semantics=sem, vmem_limit_bytes=VMEM_LIMIT_BYTES)


def _pick(dim, pref, align=LANES):
    if dim <= pref:
        return dim
    t = (pref // align) * align
    while t >= align:
        if dim % t == 0:
            return t
        t -= align
    return dim


def _silu(v):
    return v * jax.nn.sigmoid(v)


def _adaln_body(c_ref, w_ref, b_ref, o_ref):
    c = c_ref[...]
    bp = c.shape[0]
    sc = _silu(c)
    hi = sc.astype(BF16).astype(F32)
    lhs = jnp.concatenate([hi, sc - hi], axis=0).astype(BF16)
    acc = jnp.dot(lhs, w_ref[...].astype(BF16), preferred_element_type=F32)
    o_ref[...] = acc[:bp] + acc[bp:] + b_ref[...]


def adaln(c, w_ada, b_ada):
    B, D = c.shape
    N = w_ada.shape[1]
    bp = -(-B // 8) * 8
    cp = jnp.pad(c, ((0, bp - B), (0, 0)))
    tn = _pick(N, 512)
    out = pl.pallas_call(
        _adaln_body,
        grid=(N // tn,),
        in_specs=[pl.BlockSpec((bp, D), lambda j: (0, 0)),
                  pl.BlockSpec((D, tn), lambda j: (0, j)),
                  pl.BlockSpec((1, tn), lambda j: (0, j))],
        out_specs=pl.BlockSpec((bp, tn), lambda j: (0, j)),
        out_shape=jax.ShapeDtypeStruct((bp, N), F32),
        compiler_params=_params("arbitrary"),
        name="adaln",
    )(cp, w_ada, b_ada.reshape(1, N))
    return out[:B]


def _modulated_norm(x, g, shift, scale):
    ms = jnp.mean(x * x, axis=-1, keepdims=True)
    return x * lax.rsqrt(ms + NORM_EPS) * g * (1.0 + scale) + shift


def _modulate_body(x_ref, g_ref, mod_ref, o_ref, *, shift_row, scale_row):
    h = _modulated_norm(x_ref[...], g_ref[...], mod_ref[shift_row:shift_row + 1, :],
                        mod_ref[scale_row:scale_row + 1, :])
    o_ref[...] = h.astype(o_ref.dtype)


def modulate(x, g, mod3, shift_row, scale_row):
    B, S, D = x.shape
    ts = _pick(S, 512, 8)
    return pl.pallas_call(
        functools.partial(_modulate_body, shift_row=shift_row, scale_row=scale_row),
        grid=(B, S // ts),
        in_specs=[pl.BlockSpec((None, ts, D), lambda b, i: (b, i, 0)),
                  pl.BlockSpec((1, D), lambda b, i: (0, 0)),
                  pl.BlockSpec((None, 6, D), lambda b, i: (b, 0, 0))],
        out_specs=pl.BlockSpec((None, ts, D), lambda b, i: (b, i, 0)),
        out_shape=jax.ShapeDtypeStruct((B, S, D), BF16),
        compiler_params=_params("parallel", "parallel"),
        name="modulate",
    )(x, g.reshape(1, D), mod3)


PERM_BLOCK = 256


def _stage_weights(w_refs, perm_ref, wb_sc):
    @pl.when(pl.program_id(1) == 0)
    def _():
        r0 = 0
        for w_ref in w_refs:
            rows = slice(r0, r0 + w_ref.shape[0])
            r0 += w_ref.shape[0]
            if perm_ref is None:
                wb_sc[rows, :] = w_ref[...].astype(BF16)
            else:
                for c in range(w_ref.shape[1] // PERM_BLOCK):
                    cols = slice(c * PERM_BLOCK, (c + 1) * PERM_BLOCK)
                    wb_sc[rows, cols] = jnp.dot(w_ref[:, cols].astype(BF16), perm_ref[...],
                                                preferred_element_type=F32).astype(BF16)


def _proj_attn_body(h_ref, w_ref, perm_ref, ones_ref, g_ref, cs_ref, cos_ref, sin_ref, o_ref, wb_sc):
    _stage_weights([w_ref], perm_ref, wb_sc)
    acc = jnp.dot(h_ref[...], wb_sc[...], preferred_element_type=F32)
    d = ATTN_HEAD_DIM
    ssq = jnp.dot((acc * acc).astype(BF16), ones_ref[...], preferred_element_type=F32)
    y = acc * lax.rsqrt(ssq * (1.0 / d) + NORM_EPS) * g_ref[...]
    cosv = cos_ref[...]
    sinv = sin_ref[...]
    for hh in range(acc.shape[1] // d):
        sl = slice(hh * d, (hh + 1) * d)
        yh = y[:, sl]
        rot = pltpu.roll(yh, d // 2, axis=1)
        o_ref[:, sl] = ((yh * cosv + rot * sinv) * cs_ref[:, sl]).astype(o_ref.dtype)


def _proj_ret_body(h_ref, w_ref, perm_ref, cs_ref, cos_ref, sin_ref, o_ref, wb_sc):
    _stage_weights([w_ref], perm_ref, wb_sc)
    acc = jnp.dot(h_ref[...], wb_sc[...], preferred_element_type=F32)
    cosv = cos_ref[...]
    sinv = sin_ref[...]
    d = RET_HEAD_DIM
    for hh in range(acc.shape[1] // d):
        s1 = slice(hh * d, hh * d + d // 2)
        s2 = slice(hh * d + d // 2, (hh + 1) * d)
        x1 = acc[:, s1]
        x2 = acc[:, s2]
        o_ref[:, s1] = ((x1 * cosv - x2 * sinv) * cs_ref[:, s1]).astype(o_ref.dtype)
        o_ref[:, s2] = ((x1 * sinv + x2 * cosv) * cs_ref[:, s2]).astype(o_ref.dtype)


def _proj_plain_body(h_ref, w_ref, o_ref, wb_sc):
    _stage_weights([w_ref], None, wb_sc)
    o_ref[...] = jnp.dot(h_ref[...], wb_sc[...], preferred_element_type=F32).astype(o_ref.dtype)


def _proj_call(body, h2d, w, n_out, tn, in_block, mats, rows, tables, S, name):
    T, K = h2d.shape
    tm = _pick(S, 1024, 8)
    per_b = S // tm
    in_specs = [pl.BlockSpec((tm, K), lambda j, i: (i, 0)),
                pl.BlockSpec((K, tn), lambda j, i: (0, in_block(j)))]
    in_specs += [pl.BlockSpec(m.shape, lambda j, i: (0, 0)) for m in mats]
    in_specs += [pl.BlockSpec((1, tn), lambda j, i: (0, j)) for _ in rows]
    in_specs += [pl.BlockSpec((tm, t.shape[1]), lambda j, i: (i % per_b, 0)) for t in tables]
    return pl.pallas_call(
        body,
        grid=(n_out // tn, T // tm),
        in_specs=in_specs,
        out_specs=pl.BlockSpec((tm, tn), lambda j, i: (i, j)),
        out_shape=jax.ShapeDtypeStruct((T, n_out), BF16),
        scratch_shapes=[pltpu.VMEM((K, tn), BF16)],
        compiler_params=_params("arbitrary", "arbitrary"),
        name=name,
    )(h2d, w, *mats, *rows, *tables)


def _rope_tables(S, d):
    quarter = d // 4
    t = np.arange(S)
    inv = ROPE_THETA ** (-np.arange(quarter, dtype=np.float64) / quarter)
    ang = np.concatenate([(t // GRID_W)[:, None] * inv, (t % GRID_W)[:, None] * inv], axis=-1)
    return np.cos(ang).astype(np.float32), np.sin(ang).astype(np.float32)


def _deinterleave_vec(g, d):
    return g.reshape(-1, d // 2, 2).transpose(0, 2, 1).reshape(-1)


def _deinterleave_matrix(d):
    src = np.concatenate([h * d + np.concatenate([np.arange(0, d, 2), np.arange(1, d, 2)])
                          for h in range(PERM_BLOCK // d)])
    p = np.zeros((PERM_BLOCK, PERM_BLOCK), np.float32)
    p[src, np.arange(PERM_BLOCK)] = 1.0
    return jnp.asarray(p, BF16)


ATTN_TQ = 256
ATTN_TK = 512
ONES_ROWS = 16
ATTN_SAFE_EXP = 50.0


def _attn_body(q_ref, k_ref, v_ref, o_ref, vt_sc, qt_sc, kmax_sc, m_sc, acc_sc, *, tk, groups):
    tq = q_ref.shape[0]
    S, dh = k_ref.shape
    nk = S // tk

    @pl.when(pl.program_id(2) == 0)
    def _():
        for j in range(nk):
            vt_sc[j, :dh, :] = v_ref[j * tk:(j + 1) * tk, :].astype(F32).T.astype(BF16)
            vt_sc[j, dh:, :] = jnp.ones((ONES_ROWS, tk), BF16)
        kf = k_ref[...].astype(F32)
        ksq = jnp.max(jnp.sum(kf * kf, axis=1, keepdims=True), axis=0, keepdims=True)
        kmax_sc[...] = jnp.broadcast_to(jnp.sqrt(ksq), kmax_sc.shape)

    for g in range(groups):
        qt = q_ref[:, g * dh:(g + 1) * dh].astype(F32).T
        qt_sc[:, g * tq:(g + 1) * tq] = qt.astype(BF16)
        m_sc[:, g * tq:(g + 1) * tq] = jnp.sqrt(jnp.sum(qt * qt, axis=0, keepdims=True)) * kmax_sc[:, 0:1]
    acc_sc[...] = jnp.zeros(acc_sc.shape, F32)
    bounded = jnp.max(m_sc[...]) <= ATTN_SAFE_EXP

    @pl.when(bounded)
    def _():
        def step(j, carry):
            off = pl.multiple_of(j * tk, tk)
            st = jnp.dot(k_ref[pl.ds(off, tk), :], qt_sc[...], preferred_element_type=F32)
            p = jnp.exp2(st - m_sc[...]).astype(BF16)
            acc_sc[...] += jnp.dot(vt_sc[j], p, preferred_element_type=F32)
            return carry

        lax.fori_loop(0, nk, step, 0, unroll=True)

    @pl.when(jnp.logical_not(bounded))
    def _():
        m_sc[...] = jnp.full(m_sc.shape, -jnp.inf, F32)

        def step(j, carry):
            off = pl.multiple_of(j * tk, tk)
            st = jnp.dot(k_ref[pl.ds(off, tk), :], qt_sc[...], preferred_element_type=F32)
            m_prev = m_sc[...]
            m_new = jnp.maximum(m_prev, jnp.max(st, axis=0, keepdims=True))
            p = jnp.exp2(st - m_new).astype(BF16)
            acc_sc[...] = acc_sc[...] * jnp.exp2(m_prev - m_new) + jnp.dot(vt_sc[j], p, preferred_element_type=F32)
            m_sc[...] = m_new
            return carry

        lax.fori_loop(0, nk, step, 0)

    ot = acc_sc[:dh, :] / acc_sc[dh:dh + 1, :]
    for g in range(groups):
        o_ref[:, g * dh:(g + 1) * dh] = ot[:, g * tq:(g + 1) * tq].T.astype(o_ref.dtype)


def attention(qk, vg, n_heads):
    B, S, _ = qk.shape
    dh = ATTN_HEAD_DIM
    kvh = ATTN_KV_HEADS
    groups = n_heads // kvh
    tq = _pick(S, ATTN_TQ, 8)
    tk = _pick(S, ATTN_TK, 8)
    gw = groups * dh
    return pl.pallas_call(
        functools.partial(_attn_body, tk=tk, groups=groups),
        grid=(B, kvh, S // tq),
        in_specs=[pl.BlockSpec((None, tq, gw), lambda b, k, i: (b, i, k)),
                  pl.BlockSpec((None, S, dh), lambda b, k, i: (b, 0, n_heads + k)),
                  pl.BlockSpec((None, S, dh), lambda b, k, i: (b, 0, k))],
        out_specs=pl.BlockSpec((None, tq, gw), lambda b, k, i: (b, i, k)),
        out_shape=jax.ShapeDtypeStruct((B, S, n_heads * dh), BF16),
        scratch_shapes=[pltpu.VMEM((S // tk, dh + ONES_ROWS, tk), BF16),
                        pltpu.VMEM((dh, groups * tq), BF16),
                        pltpu.VMEM((1, LANES), F32),
                        pltpu.VMEM((1, groups * tq), F32),
                        pltpu.VMEM((dh + ONES_ROWS, groups * tq), F32)],
        compiler_params=_params("arbitrary", "arbitrary", "arbitrary"),
        name="attention",
    )(qk, qk, vg)


RET_UNROLL = 8


def _ret_body(dec_ref, q_ref, k_ref, v_ref, gr_ref, g_ref, o_ref, rf_sc, rb_sc, racc_sc, *, chunk):
    C = chunk
    S, dk = q_ref.shape
    n = S // C
    h = pl.program_id(1)
    lgf = -jnp.exp(jnp.full((1, 1), dec_ref[0, h], F32))
    lgb = -jnp.exp(jnp.full((1, 1), dec_ref[1, h], F32))
    idx = lax.broadcasted_iota(jnp.int32, (C, 1), 0).astype(F32)
    zeta_f = jnp.exp(lgf * (C - 1.0 - idx))
    xi_f = jnp.exp(lgf * (idx + 1.0))
    zeta_b = jnp.exp(lgb * idx)
    xi_b = jnp.exp(lgb * (C - idx))
    cd_f = jnp.exp(lgf * C)
    cd_b = jnp.exp(lgb * C)
    diff = (lax.broadcasted_iota(jnp.int32, (C, C), 0) - lax.broadcasted_iota(jnp.int32, (C, C), 1)).astype(F32)
    decay = jnp.where(diff >= 0, jnp.exp(lgf * jnp.maximum(diff, 0.0)), jnp.exp(lgb * jnp.maximum(-diff, 0.0)))

    def kv_update(i, zeta, cd):
        off = pl.multiple_of(i * C, C)
        kz = (k_ref[pl.ds(off, C), :].astype(F32) * zeta).astype(BF16)
        kv = lax.dot_general(kz, v_ref[pl.ds(off, C), :], TN_DIMS, preferred_element_type=F32)
        racc_sc[...] = racc_sc[...] * cd + kv

    racc_sc[...] = jnp.zeros(racc_sc.shape, F32)

    def fstep(i, carry):
        rf_sc[i] = racc_sc[...].astype(BF16)
        kv_update(i, zeta_f, cd_f)
        return carry

    lax.fori_loop(0, n, fstep, 0, unroll=RET_UNROLL)
    racc_sc[...] = jnp.zeros(racc_sc.shape, F32)

    def bstep(t, carry):
        i = n - 1 - t
        rb_sc[i] = racc_sc[...].astype(BF16)
        kv_update(i, zeta_b, cd_b)
        return carry

    lax.fori_loop(0, n, bstep, 0, unroll=RET_UNROLL)
    gain = g_ref[...]

    def ostep(i, carry):
        off = pl.multiple_of(i * C, C)
        qi = q_ref[pl.ds(off, C), :]
        ki = k_ref[pl.ds(off, C), :]
        vi = v_ref[pl.ds(off, C), :]
        s = lax.dot_general(qi, ki, NT_DIMS, preferred_element_type=F32) * decay
        qf = (qi.astype(F32) * xi_f).astype(BF16)
        qb = (qi.astype(F32) * xi_b).astype(BF16)
        o = (jnp.dot(s.astype(BF16), vi, preferred_element_type=F32)
             + jnp.dot(qf, rf_sc[i], preferred_element_type=F32)
             + jnp.dot(qb, rb_sc[i], preferred_element_type=F32))
        mu = jnp.mean(o, axis=-1, keepdims=True)
        d = o - mu
        var = jnp.mean(d * d, axis=-1, keepdims=True)
        y = d * lax.rsqrt(var + NORM_EPS) * gain
        o_ref[pl.ds(off, C), :] = (y * _silu(gr_ref[pl.ds(off, C), :].astype(F32))).astype(o_ref.dtype)
        return carry

    lax.fori_loop(0, n, ostep, 0, unroll=RET_UNROLL)


def retention(qk, vg, decays, gain, n_heads, v_off, g_off):
    B, S, _ = qk.shape
    d = RET_HEAD_DIM
    n = S // RET_CHUNK
    blk = lambda off: pl.BlockSpec((None, S, d), lambda b, h: (b, 0, off + h))
    return pl.pallas_call(
        functools.partial(_ret_body, chunk=RET_CHUNK),
        grid=(B, n_heads),
        in_specs=[pl.BlockSpec(memory_space=pltpu.SMEM),
                  blk(0), blk(n_heads), blk(v_off), blk(g_off),
                  pl.BlockSpec((None, 1, d), lambda b, h: (h, 0, 0))],
        out_specs=blk(0),
        out_shape=jax.ShapeDtypeStruct((B, S, n_heads * d), BF16),
        scratch_shapes=[pltpu.VMEM((n, d, d), BF16), pltpu.VMEM((n, d, d), BF16), pltpu.VMEM((d, d), F32)],
        compiler_params=_params("parallel", "parallel"),
        name="retention",
    )(decays, qk, qk, vg, vg, gain.reshape(n_heads, 1, d))


def _outproj_body(a_ref, r_ref, wa_ref, wr_ref, x_ref, mod_ref, o_ref, wb_sc, *, gate_row):
    _stage_weights([wa_ref, wr_ref], None, wb_sc)
    ka = a_ref.shape[1]
    acc = (jnp.dot(a_ref[...], wb_sc[:ka, :], preferred_element_type=F32)
           + jnp.dot(r_ref[...], wb_sc[ka:, :], preferred_element_type=F32))
    o_ref[...] = x_ref[...] + mod_ref[gate_row:gate_row + 1, :] * acc


def outproj(attn_out, ret_out, w_o, x2d, mod3, S, gate_row):
    T, Ka = attn_out.shape
    Kr = ret_out.shape[1]
    assert Ka == Kr and w_o.shape[0] == Ka + Kr
    N = w_o.shape[1]
    tm = _pick(S, 1024, 8)
    tn = _pick(N, 512, 2 * LANES)
    per_b = S // tm
    return pl.pallas_call(
        functools.partial(_outproj_body, gate_row=gate_row),
        grid=(N // tn, T // tm),
        in_specs=[pl.BlockSpec((tm, Ka), lambda j, i: (i, 0)),
                  pl.BlockSpec((tm, Kr), lambda j, i: (i, 0)),
                  pl.BlockSpec((Ka, tn), lambda j, i: (0, j)),
                  pl.BlockSpec((Kr, tn), lambda j, i: (1, j)),
                  pl.BlockSpec((tm, tn), lambda j, i: (i, j)),
                  pl.BlockSpec((None, 6, tn), lambda j, i: (i // per_b, 0, j))],
        out_specs=pl.BlockSpec((tm, tn), lambda j, i: (i, j)),
        out_shape=jax.ShapeDtypeStruct((T, N), F32),
        scratch_shapes=[pltpu.VMEM((Ka + Kr, tn), BF16)],
        compiler_params=_params("arbitrary", "arbitrary"),
        name="outproj",
    )(attn_out, ret_out, w_o, w_o, x2d, mod3)


def _router_body(x_ref, g_ref, mod_ref, wr_ref, h_ref, aff_ref, *, shift_row, scale_row):
    h = _modulated_norm(x_ref[...], g_ref[...], mod_ref[shift_row:shift_row + 1, :],
                        mod_ref[scale_row:scale_row + 1, :])
    h_ref[...] = h
    hi = h.astype(BF16)
    lo = (h - hi.astype(F32)).astype(BF16)
    whi = wr_ref[0]
    wlo = wr_ref[1]
    lg = (lax.dot_general(whi, hi, NT_DIMS, preferred_element_type=F32)
          + lax.dot_general(whi, lo, NT_DIMS, preferred_element_type=F32)
          + lax.dot_general(wlo, hi, NT_DIMS, preferred_element_type=F32))
    e = jnp.exp(lg - jnp.max(lg, axis=0, keepdims=True))
    aff_ref[...] = e / jnp.sum(e, axis=0, keepdims=True)


def router(x1, g, mod3, w_router, shift_row, scale_row):
    B, S, D = x1.shape
    E = w_router.shape[1]
    wt = w_router.T
    whi = wt.astype(BF16)
    wsplit = jnp.stack([whi, (wt - whi.astype(F32)).astype(BF16)])
    ts = _pick(S, 512)
    return pl.pallas_call(
        functools.partial(_router_body, shift_row=shift_row, scale_row=scale_row),
        grid=(B, S // ts),
        in_specs=[pl.BlockSpec((None, ts, D), lambda b, i: (b, i, 0)),
                  pl.BlockSpec((1, D), lambda b, i: (0, 0)),
                  pl.BlockSpec((None, 6, D), lambda b, i: (b, 0, 0)),
                  pl.BlockSpec((2, E, D), lambda b, i: (0, 0, 0))],
        out_specs=[pl.BlockSpec((None, ts, D), lambda b, i: (b, i, 0)),
                   pl.BlockSpec((None, E, ts), lambda b, i: (b, 0, i))],
        out_shape=[jax.ShapeDtypeStruct((B, S, D), F32), jax.ShapeDtypeStruct((B, E, S), F32)],
        compiler_params=_params("parallel", "parallel"),
        name="router",
    )(x1, g.reshape(1, D), mod3, wsplit)


def _cumsum_lanes(x):
    rows, S = x.shape
    tri = (lax.broadcasted_iota(jnp.int32, (LANES, LANES), 0)
           <= lax.broadcasted_iota(jnp.int32, (LANES, LANES), 1))
    tri = jnp.where(tri, 1.0, 0.0).astype(BF16)
    carry = jnp.zeros((rows, 1), F32)
    outs = []
    for c in range(S // LANES):
        inc = jnp.dot(x[:, c * LANES:(c + 1) * LANES].astype(BF16), tri, preferred_element_type=F32) + carry
        outs.append(inc)
        carry = inc[:, LANES - 1:LANES]
    return jnp.concatenate(outs, axis=1)


COMBINE_TM = 128
SLOT_CHUNK = 16
KBLOCK = 256


def _select_body(aff_ref, pos_ref, idx_ref, ts_ref, tc_ref, *, cap, tm):
    aff = aff_ref[...]
    E, S = aff.shape

    def count(mask):
        return jnp.sum(jnp.where(mask, 1.0, 0.0), axis=1, keepdims=True)

    def search(i, t):
        cand = t | jnp.left_shift(jnp.int32(1), 30 - i)
        return jnp.where(count(aff >= pltpu.bitcast(cand, F32)) >= cap, cand, t)

    thr = pltpu.bitcast(lax.fori_loop(0, 31, search, jnp.zeros((E, 1), jnp.int32)), F32)
    gt = aff > thr
    eq = aff == thr
    need = cap - count(gt)
    eq_rank = _cumsum_lanes(jnp.where(eq, 1.0, 0.0))
    sel = gt | (eq & (eq_rank <= need))
    rank = _cumsum_lanes(jnp.where(sel, 1.0, 0.0))
    sel = sel & (rank <= cap)
    sel_f = jnp.where(sel, 1.0, 0.0)
    pos = jnp.where(sel, rank - 1.0, -1.0).astype(jnp.int32)
    pos_ref[...] = pos

    tile_of = lax.broadcasted_iota(jnp.int32, (S, LANES), 0) // tm
    lane = lax.broadcasted_iota(jnp.int32, (S, LANES), 1)
    cnt = jnp.dot(sel_f.astype(BF16), jnp.where(tile_of == lane, 1.0, 0.0).astype(BF16),
                  preferred_element_type=F32)
    r = lax.broadcasted_iota(jnp.int32, (LANES, LANES), 0)
    c = lax.broadcasted_iota(jnp.int32, (LANES, LANES), 1)
    start = jnp.dot(cnt.astype(BF16), jnp.where(r < c, 1.0, 0.0).astype(BF16), preferred_element_type=F32)
    tc_ref[...] = cnt.astype(jnp.int32)
    ts_ref[...] = start.astype(jnp.int32)

    slot = lax.broadcasted_iota(jnp.int32, (cap, S), 0)
    tok = (lax.broadcasted_iota(jnp.int32, (1, S), 1) + pl.program_id(0) * S).astype(F32)
    for e in range(E):
        col = jnp.sum(jnp.where(pos[e:e + 1, :] == slot, tok, 0.0), axis=1, keepdims=True)
        idx_ref[e:e + 1, :] = jnp.broadcast_to(col, (cap, LANES)).T[0:1, :].astype(jnp.int32)


def select(aff_t, cap, tm):
    B, E, S = aff_t.shape
    assert S // tm <= LANES and tm <= 256
    blk = lambda n: pl.BlockSpec((None, E, n), lambda b: (b, 0, 0))
    return pl.pallas_call(
        functools.partial(_select_body, cap=cap, tm=tm),
        grid=(B,),
        in_specs=[blk(S)],
        out_specs=[blk(S), blk(cap), blk(LANES), blk(LANES)],
        out_shape=[jax.ShapeDtypeStruct((B, E, S), jnp.int32), jax.ShapeDtypeStruct((B, E, cap), jnp.int32),
                   jax.ShapeDtypeStruct((B, E, LANES), jnp.int32), jax.ShapeDtypeStruct((B, E, LANES), jnp.int32)],
        compiler_params=_params("parallel"),
        name="select",
    )(aff_t)


def _gather_rows_body(idx_ref, h_hbm, xe_ref, stage, sems, *, cap):
    g = pl.program_id(0)
    slot = g % 2

    def row_copy(slot, p, row):
        return pltpu.make_async_copy(h_hbm.at[pl.ds(row, 1), :], stage.at[slot, pl.ds(p, 1), :], sems.at[slot])

    def issue_group(grp, slot):
        def issue(p, carry):
            row_copy(slot, p, idx_ref[grp * cap + p]).start()
            return carry

        lax.fori_loop(0, cap, issue, 0, unroll=8)

    @pl.when(g == 0)
    def _():
        issue_group(0, 0)

    @pl.when(g + 1 < pl.num_programs(0))
    def _():
        issue_group(g + 1, 1 - slot)

    def drain(p, carry):
        row_copy(slot, p, 0).wait()
        return carry

    lax.fori_loop(0, cap, drain, 0, unroll=8)
    xe_ref[...] = stage[slot].astype(xe_ref.dtype)


def gather_rows(idx, h2, cap):
    B, E, _ = idx.shape
    D = h2.shape[2]
    h2 = h2.reshape(-1, D)
    return pl.pallas_call(
        functools.partial(_gather_rows_body, cap=cap),
        grid_spec=pltpu.PrefetchScalarGridSpec(
            num_scalar_prefetch=1,
            grid=(B * E,),
            in_specs=[pl.BlockSpec(memory_space=pl.ANY)],
            out_specs=pl.BlockSpec((None, cap, D), lambda g, idx: (g % E, g // E, 0)),
            scratch_shapes=[pltpu.VMEM((2, cap, D), F32), pltpu.SemaphoreType.DMA((2,))]),
        out_shape=jax.ShapeDtypeStruct((E, B * cap, D), BF16),
        compiler_params=_params("arbitrary"),
        name="gather_rows",
    )(idx.reshape(-1), h2)


def _up_body(x_ref, w1_ref, w3_ref, o_ref):
    x = x_ref[...]
    a = jnp.dot(x, w1_ref[...].astype(BF16), preferred_element_type=F32)
    b = jnp.dot(x, w3_ref[...].astype(BF16), preferred_element_type=F32)
    o_ref[...] = (_silu(a) * b).astype(o_ref.dtype)


def expert_up(xe, w1, w3):
    E, M, D = xe.shape
    Fd = w1.shape[2]
    tm = _pick(M, 1024, 8)
    tf = _pick(Fd, 256, 2 * LANES)
    return pl.pallas_call(
        _up_body,
        grid=(E, M // tm, Fd // tf),
        in_specs=[pl.BlockSpec((None, tm, D), lambda e, m, f: (e, m, 0)),
                  pl.BlockSpec((None, D, tf), lambda e, m, f: (e, 0, f)),
                  pl.BlockSpec((None, D, tf), lambda e, m, f: (e, 0, f))],
        out_specs=pl.BlockSpec((None, tm, tf), lambda e, m, f: (e, m, f)),
        out_shape=jax.ShapeDtypeStruct((E, M, Fd), BF16),
        compiler_params=_params("parallel", "parallel", "arbitrary"),
        name="expert_up",
    )(xe, w1, w3)


def _down_body(h_ref, w2_ref, o_ref):
    o_ref[...] = jnp.dot(h_ref[...], w2_ref[...].astype(BF16), preferred_element_type=F32).astype(o_ref.dtype)


def expert_down(hid, w2):
    E, M, Fd = hid.shape
    D = w2.shape[2]
    tm = _pick(M, 2048, 8)
    tn = _pick(D, 512, 2 * LANES)
    return pl.pallas_call(
        _down_body,
        grid=(E, M // tm, D // tn),
        in_specs=[pl.BlockSpec((None, tm, Fd), lambda e, m, n: (e, m, 0)),
                  pl.BlockSpec((None, Fd, tn), lambda e, m, n: (e, 0, n))],
        out_specs=pl.BlockSpec((None, tm, tn), lambda e, m, n: (e, m, n)),
        out_shape=jax.ShapeDtypeStruct((E, M, D), BF16),
        compiler_params=_params("parallel", "parallel", "arbitrary"),
        name="expert_down",
    )(hid, w2)


def _combine_body(ts_ref, tc_ref, pos_ref, aff_ref, ye_hbm, x_ref, mod_ref, g_ref, o_ref,
                  buf, wt, acc_sc, sems, *, cap, gate_row):
    b = pl.program_id(0)
    t = pl.program_id(1)
    n_b = pl.num_programs(0)
    n_t = pl.num_programs(1)
    E, tm = pos_ref.shape
    nt_pad = ts_ref.shape[0] // (n_b * E)
    C = SLOT_CHUNK
    shift = C.bit_length() - 1
    step = b * n_t + t
    slot = step % 2

    def chunk_copy(e, row0, slot_, k):
        return pltpu.make_async_copy(ye_hbm.at[e, pl.ds(row0, C), :], buf.at[slot_, pl.ds(k, C), :], sems.at[slot_])

    def for_each_chunk(bb, tt, fn):
        k = jnp.int32(0)
        for e in range(E):
            base = (bb * E + e) * nt_pad + tt
            first = ts_ref[base]
            n = tc_ref[base]
            first_al = lax.shift_left(lax.shift_right_logical(first, shift), shift)
            n_chunks = jnp.where(n > 0, lax.shift_right_logical(first + n - first_al + (C - 1), shift), 0)

            def chunk(c, k, e=e, first_al=first_al):
                k = pl.multiple_of(k, C)
                fn(e, first_al + c * C, k)
                return k + C

            k = lax.fori_loop(0, n_chunks, chunk, k)
        return k

    def start_tile(bb, tt, slot_):
        for_each_chunk(bb, tt, lambda e, p0, k: chunk_copy(e, pl.multiple_of(bb * cap + p0, C), slot_, k).start())

    @pl.when(step == 0)
    def _():
        buf[...] = jnp.zeros(buf.shape, buf.dtype)
        start_tile(0, 0, 0)

    @pl.when(step + 1 < n_b * n_t)
    def _():
        wrap = t + 1 == n_t
        start_tile(jnp.where(wrap, b + 1, b), jnp.where(wrap, 0, t + 1), 1 - slot)

    def weights(e, p0, k):
        rows = p0 + lax.broadcasted_iota(jnp.int32, (C, tm), 0)
        wt[pl.ds(k, C), :] = jnp.where(pos_ref[e:e + 1, :] == rows, aff_ref[e:e + 1, :], 0.0).astype(BF16)

    k = for_each_chunk(b, t, weights)
    n_used = lax.shift_right_logical(k, shift)
    n_blocks = lax.shift_right_logical(k + (KBLOCK - 1), KBLOCK.bit_length() - 1)

    def zero_tail(c, carry):
        wt[pl.ds(pl.multiple_of(c * C, C), C), :] = jnp.zeros((C, tm), BF16)
        return carry

    lax.fori_loop(n_used, n_blocks * (KBLOCK // C), zero_tail, 0)

    def drain(c, carry):
        chunk_copy(0, 0, slot, 0).wait()
        return carry

    lax.fori_loop(0, n_used, drain, 0)
    acc_sc[...] = jnp.zeros(acc_sc.shape, F32)

    def block(kb, carry):
        r0 = pl.multiple_of(kb * KBLOCK, KBLOCK)
        acc_sc[...] += lax.dot_general(wt[pl.ds(r0, KBLOCK), :], buf[slot, pl.ds(r0, KBLOCK), :], TN_DIMS,
                                       preferred_element_type=F32)
        return carry

    lax.fori_loop(0, n_blocks, block, 0)
    x2 = x_ref[...] + mod_ref[gate_row:gate_row + 1, :] * acc_sc[...]
    ms = jnp.mean(x2 * x2, axis=-1, keepdims=True)
    o_ref[...] = x2 * lax.rsqrt(ms + NORM_EPS) * g_ref[...]


def combine(tstart, tcount, pos, aff_t, ye, x1, mod3, final_g, cap, gate_row):
    B, E, S = pos.shape
    D = x1.shape[2]
    tm = min(COMBINE_TM, S)
    kcap = -(-E * (tm + SLOT_CHUNK) // KBLOCK) * KBLOCK
    return pl.pallas_call(
        functools.partial(_combine_body, cap=cap, gate_row=gate_row),
        grid_spec=pltpu.PrefetchScalarGridSpec(
            num_scalar_prefetch=2,
            grid=(B, S // tm),
            in_specs=[pl.BlockSpec((None, E, tm), lambda b, t, *_: (b, 0, t)),
                      pl.BlockSpec((None, E, tm), lambda b, t, *_: (b, 0, t)),
                      pl.BlockSpec(memory_space=pl.ANY),
                      pl.BlockSpec((None, tm, D), lambda b, t, *_: (b, t, 0)),
                      pl.BlockSpec((None, 6, D), lambda b, t, *_: (b, 0, 0)),
                      pl.BlockSpec((1, D), lambda b, t, *_: (0, 0))],
            out_specs=pl.BlockSpec((None, tm, D), lambda b, t, *_: (b, t, 0)),
            scratch_shapes=[pltpu.VMEM((2, kcap, D), BF16), pltpu.VMEM((kcap, tm), BF16),
                            pltpu.VMEM((tm, D), F32), pltpu.SemaphoreType.DMA((2,))]),
        out_shape=jax.ShapeDtypeStruct((B, S, D), F32),
        compiler_params=_params("arbitrary", "arbitrary"),
        name="combine",
    )(tstart.reshape(-1), tcount.reshape(-1), pos, aff_t, ye, x1, mod3, final_g.reshape(1, D))


def kernel(x, c, w_ada, b_ada, norm1_g, w_in, attn_q_norm_g, attn_k_norm_g, ret_decay_fwd, ret_decay_bwd,
           ret_norm_g, w_out, norm2_g, w_router, w1, w3, w2, final_g):
    B, S, D = x.shape
    T = B * S
    mix = w_out.shape[0]
    attn_w = mix // 2
    ret_w = mix - attn_w
    ah = attn_w // ATTN_HEAD_DIM
    kv_w = ATTN_KV_HEADS * ATTN_HEAD_DIM
    rh = ret_w // RET_HEAD_DIM
    E = w_router.shape[1]
    cap = CAPACITY_FACTOR * S // E

    o1 = attn_w + kv_w
    o2 = o1 + kv_w
    o3 = o2 + 2 * ret_w
    gq = _deinterleave_vec(attn_q_norm_g, ATTN_HEAD_DIM)
    gk = _deinterleave_vec(attn_k_norm_g, ATTN_HEAD_DIM)
    gain_a = jnp.concatenate([jnp.tile(gq, ah), jnp.tile(gk, ATTN_KV_HEADS)]).reshape(1, o1)
    q_scale = math.log2(math.e) / math.sqrt(ATTN_HEAD_DIM)
    cs_a = jnp.concatenate([jnp.full((attn_w,), q_scale, F32), jnp.ones((kv_w,), F32)]).reshape(1, o1)
    cs_b = jnp.concatenate([jnp.ones((ret_w,), F32), jnp.full((ret_w,), RET_HEAD_DIM ** -0.5, F32)]).reshape(1, 2 * ret_w)
    cos_a, sin_a = _rope_tables(S, ATTN_HEAD_DIM)
    cos_a2 = jnp.asarray(np.concatenate([cos_a, cos_a], axis=1))
    sin_a2 = jnp.asarray(np.concatenate([-sin_a, sin_a], axis=1))
    cos_r, sin_r = (jnp.asarray(t) for t in _rope_tables(S, RET_HEAD_DIM))

    mod3 = adaln(c, w_ada, b_ada).reshape(B, 6, D)

    h = modulate(x, norm1_g, mod3, 0, 1).reshape(T, D)
    tn = _pick(kv_w, 512, PERM_BLOCK)
    head_of = np.arange(tn) // ATTN_HEAD_DIM
    head_ones = jnp.asarray(head_of[:, None] == head_of[None, :], BF16)
    qk_a = _proj_call(_proj_attn_body, h, w_in, o1, tn, lambda j: j,
                      [_deinterleave_matrix(ATTN_HEAD_DIM), head_ones], [gain_a, cs_a], [cos_a2, sin_a2], S, "proj_attn_qk")
    qk_r = _proj_call(_proj_ret_body, h, w_in, 2 * ret_w, tn, lambda j: o2 // tn + j,
                      [_deinterleave_matrix(RET_HEAD_DIM)], [cs_b], [cos_r, sin_r], S, "proj_ret_qk")
    vg = _proj_call(_proj_plain_body, h, w_in, kv_w + 2 * ret_w, tn,
                    lambda j: jnp.where(j < kv_w // tn, o1 // tn + j, (o3 - kv_w) // tn + j), [], [], [], S, "proj_vg")
    attn_out = attention(qk_a.reshape(B, S, o1), vg.reshape(B, S, -1), ah)
    decays = jnp.stack([ret_decay_fwd, ret_decay_bwd]).astype(F32)
    v_off = kv_w // RET_HEAD_DIM
    ret_out = retention(qk_r.reshape(B, S, 2 * ret_w), vg.reshape(B, S, -1), decays, ret_norm_g.astype(F32),
                        rh, v_off, v_off + rh)
    x1 = outproj(attn_out.reshape(T, attn_w), ret_out.reshape(T, ret_w), w_out,
                 x.reshape(T, D), mod3, S, 2).reshape(B, S, D)

    h2, aff_t = router(x1, norm2_g, mod3, w_router, 3, 4)
    pos, idx, tstart, tcount = select(aff_t, cap, min(COMBINE_TM, S))
    xe = gather_rows(idx, h2, cap)
    ye = expert_down(expert_up(xe, w1, w3), w2)
    return combine(tstart, tcount, pos, aff_t, ye, x1, mod3, final_g, cap, 5)
```

```python
import functools
import math

import numpy as np
import jax
import jax.numpy as jnp
from jax import lax
from jax.experimental import pallas as pl
from jax.experimental.pallas import tpu as pltpu

F32 = jnp.float32
BF16 = jnp.bfloat16

GRID_W = 64
ROPE_THETA = 10000.0
NORM_EPS = 1e-6
ATTN_HEAD_DIM = 128
ATTN_KV_HEADS = 4
RET_HEAD_DIM = 256
RET_CHUNK = 128
CAPACITY_FACTOR = 2

LANES = 128
VMEM_LIMIT_BYTES = 56 * 1024 * 1024

NT_DIMS = (((1,), (1,)), ((), ()))
TN_DIMS = (((0,), (0,)), ((), ()))


def _params(*sem):
    return pltpu.CompilerParams(dimension_semantics=sem, vmem_limit_bytes=VMEM_LIMIT_BYTES)


def _pick(dim, pref, align=LANES):
    if dim <= pref:
        return dim
    t = (pref // align) * align
    while t >= align:
        if dim % t == 0:
            return t
        t -= align
    return dim


def _silu(v):
    return v * jax.nn.sigmoid(v)


def _adaln_body(c_ref, w_ref, b_ref, o_ref):
    c = c_ref[...]
    bp = c.shape[0]
    sc = _silu(c)
    hi = sc.astype(BF16).astype(F32)
    lhs = jnp.concatenate([hi, sc - hi], axis=0).astype(BF16)
    acc = jnp.dot(lhs, w_ref[...].astype(BF16), preferred_element_type=F32)
    o_ref[...] = acc[:bp] + acc[bp:] + b_ref[...]


def adaln(c, w_ada, b_ada):
    B, D = c.shape
    N = w_ada.shape[1]
    bp = -(-B // 8) * 8
    cp = jnp.pad(c, ((0, bp - B), (0, 0)))
    tn = _pick(N, 512)
    out = pl.pallas_call(
        _adaln_body,
        grid=(N // tn,),
        in_specs=[pl.BlockSpec((bp, D), lambda j: (0, 0)),
                  pl.BlockSpec((D, tn), lambda j: (0, j)),
                  pl.BlockSpec((1, tn), lambda j: (0, j))],
        out_specs=pl.BlockSpec((bp, tn), lambda j: (0, j)),
        out_shape=jax.ShapeDtypeStruct((bp, N), F32),
        compiler_params=_params("arbitrary"),
        name="adaln",
    )(cp, w_ada, b_ada.reshape(1, N))
    return out[:B]


def _modulated_norm(x, g, shift, scale):
    ms = jnp.mean(x * x, axis=-1, keepdims=True)
    return x * lax.rsqrt(ms + NORM_EPS) * g * (1.0 + scale) + shift


def _modulate_body(x_ref, g_ref, mod_ref, o_ref, *, shift_row, scale_row):
    h = _modulated_norm(x_ref[...], g_ref[...], mod_ref[shift_row:shift_row + 1, :],
                        mod_ref[scale_row:scale_row + 1, :])
    o_ref[...] = h.astype(o_ref.dtype)


def modulate(x, g, mod3, shift_row, scale_row):
    B, S, D = x.shape
    ts = _pick(S, 512, 8)
    return pl.pallas_call(
        functools.partial(_modulate_body, shift_row=shift_row, scale_row=scale_row),
        grid=(B, S // ts),
        in_specs=[pl.BlockSpec((None, ts, D), lambda b, i: (b, i, 0)),
                  pl.BlockSpec((1, D), lambda b, i: (0, 0)),
                  pl.BlockSpec((None, 6, D), lambda b, i: (b, 0, 0))],
        out_specs=pl.BlockSpec((None, ts, D), lambda b, i: (b, i, 0)),
        out_shape=jax.ShapeDtypeStruct((B, S, D), BF16),
        compiler_params=_params("parallel", "parallel"),
        name="modulate",
    )(x, g.reshape(1, D), mod3)


PERM_BLOCK = 256


def _stage_weights(w_refs, perm_ref, wb_sc):
    @pl.when(pl.program_id(1) == 0)
    def _():
        r0 = 0
        for w_ref in w_refs:
            rows = slice(r0, r0 + w_ref.shape[0])
            r0 += w_ref.shape[0]
            if perm_ref is None:
                wb_sc[rows, :] = w_ref[...].astype(BF16)
            else:
                for c in range(w_ref.shape[1] // PERM_BLOCK):
                    cols = slice(c * PERM_BLOCK, (c + 1) * PERM_BLOCK)
                    wb_sc[rows, cols] = jnp.dot(w_ref[:, cols].astype(BF16), perm_ref[...],
                                                preferred_element_type=F32).astype(BF16)


def _proj_attn_body(h_ref, w_ref, perm_ref, ones_ref, g_ref, cs_ref, cos_ref, sin_ref, o_ref, wb_sc):
    _stage_weights([w_ref], perm_ref, wb_sc)
    acc = jnp.dot(h_ref[...], wb_sc[...], preferred_element_type=F32)
    d = ATTN_HEAD_DIM
    ssq = jnp.dot((acc * acc).astype(BF16), ones_ref[...], preferred_element_type=F32)
    y = acc * lax.rsqrt(ssq * (1.0 / d) + NORM_EPS) * g_ref[...]
    cosv = cos_ref[...]
    sinv = sin_ref[...]
    for hh in range(acc.shape[1] // d):
        sl = slice(hh * d, (hh + 1) * d)
        yh = y[:, sl]
        rot = pltpu.roll(yh, d // 2, axis=1)
        o_ref[:, sl] = ((yh * cosv + rot * sinv) * cs_ref[:, sl]).astype(o_ref.dtype)


def _proj_ret_body(h_ref, w_ref, perm_ref, cs_ref, cos_ref, sin_ref, o_ref, wb_sc):
    _stage_weights([w_ref], perm_ref, wb_sc)
    acc = jnp.dot(h_ref[...], wb_sc[...], preferred_element_type=F32)
    cosv = cos_ref[...]
    sinv = sin_ref[...]
    d = RET_HEAD_DIM
    for hh in range(acc.shape[1] // d):
        s1 = slice(hh * d, hh * d + d // 2)
        s2 = slice(hh * d + d // 2, (hh + 1) * d)
        x1 = acc[:, s1]
        x2 = acc[:, s2]
        o_ref[:, s1] = ((x1 * cosv - x2 * sinv) * cs_ref[:, s1]).astype(o_ref.dtype)
        o_ref[:, s2] = ((x1 * sinv + x2 * cosv) * cs_ref[:, s2]).astype(o_ref.dtype)


def _proj_plain_body(h_ref, w_ref, o_ref, wb_sc):
    _stage_weights([w_ref], None, wb_sc)
    o_ref[...] = jnp.dot(h_ref[...], wb_sc[...], preferred_element_type=F32).astype(o_ref.dtype)


def _proj_call(body, h2d, w, n_out, tn, in_block, mats, rows, tables, S, name):
    T, K = h2d.shape
    tm = _pick(S, 1024, 8)
    per_b = S // tm
    in_specs = [pl.BlockSpec((tm, K), lambda j, i: (i, 0)),
                pl.BlockSpec((K, tn), lambda j, i: (0, in_block(j)))]
    in_specs += [pl.BlockSpec(m.shape, lambda j, i: (0, 0)) for m in mats]
    in_specs += [pl.BlockSpec((1, tn), lambda j, i: (0, j)) for _ in rows]
    in_specs += [pl.BlockSpec((tm, t.shape[1]), lambda j, i: (i % per_b, 0)) for t in tables]
    return pl.pallas_call(
        body,
        grid=(n_out // tn, T // tm),
        in_specs=in_specs,
        out_specs=pl.BlockSpec((tm, tn), lambda j, i: (i, j)),
        out_shape=jax.ShapeDtypeStruct((T, n_out), BF16),
        scratch_shapes=[pltpu.VMEM((K, tn), BF16)],
        compiler_params=_params("arbitrary", "arbitrary"),
        name=name,
    )(h2d, w, *mats, *rows, *tables)


def _rope_tables(S, d):
    quarter = d // 4
    t = np.arange(S)
    inv = ROPE_THETA ** (-np.arange(quarter, dtype=np.float64) / quarter)
    ang = np.concatenate([(t // GRID_W)[:, None] * inv, (t % GRID_W)[:, None] * inv], axis=-1)
    return np.cos(ang).astype(np.float32), np.sin(ang).astype(np.float32)


def _deinterleave_vec(g, d):
    return g.reshape(-1, d // 2, 2).transpose(0, 2, 1).reshape(-1)


def _deinterleave_matrix(d):
    src = np.concatenate([h * d + np.concatenate([np.arange(0, d, 2), np.arange(1, d, 2)])
                          for h in range(PERM_BLOCK // d)])
    p = np.zeros((PERM_BLOCK, PERM_BLOCK), np.float32)
    p[src, np.arange(PERM_BLOCK)] = 1.0
    return jnp.asarray(p, BF16)


ATTN_TQ = 256
ATTN_TK = 512
ONES_ROWS = 16
ATTN_SAFE_EXP = 50.0


def _attn_body(q_ref, k_ref, v_ref, o_ref, vt_sc, qt_sc, kmax_sc, m_sc, acc_sc, *, tk, groups):
    tq = q_ref.shape[0]
    S, dh = k_ref.shape
    nk = S // tk

    @pl.when(pl.program_id(2) == 0)
    def _():
        for j in range(nk):
            vt_sc[j, :dh, :] = v_ref[j * tk:(j + 1) * tk, :].astype(F32).T.astype(BF16)
            vt_sc[j, dh:, :] = jnp.ones((ONES_ROWS, tk), BF16)
        kf = k_ref[...].astype(F32)
        ksq = jnp.max(jnp.sum(kf * kf, axis=1, keepdims=True), axis=0, keepdims=True)
        kmax_sc[...] = jnp.broadcast_to(jnp.sqrt(ksq), kmax_sc.shape)

    for g in range(groups):
        qt = q_ref[:, g * dh:(g + 1) * dh].astype(F32).T
        qt_sc[:, g * tq:(g + 1) * tq] = qt.astype(BF16)
        m_sc[:, g * tq:(g + 1) * tq] = jnp.sqrt(jnp.sum(qt * qt, axis=0, keepdims=True)) * kmax_sc[:, 0:1]
    bounded = jnp.max(m_sc[...]) <= ATTN_SAFE_EXP

    @pl.when(bounded)
    def _():
        for j in range(nk):
            st = jnp.dot(k_ref[j * tk:(j + 1) * tk, :], qt_sc[...], preferred_element_type=F32)
            p = jnp.exp2(st - m_sc[...]).astype(BF16)
            pv = jnp.dot(vt_sc[j], p, preferred_element_type=F32)
            acc_sc[...] = pv if j == 0 else acc_sc[...] + pv

    @pl.when(jnp.logical_not(bounded))
    def _():
        m_sc[...] = jnp.full(m_sc.shape, -jnp.inf, F32)
        acc_sc[...] = jnp.zeros(acc_sc.shape, F32)

        def step(j, carry):
            off = pl.multiple_of(j * tk, tk)
            st = jnp.dot(k_ref[pl.ds(off, tk), :], qt_sc[...], preferred_element_type=F32)
            m_prev = m_sc[...]
            m_new = jnp.maximum(m_prev, jnp.max(st, axis=0, keepdims=True))
            p = jnp.exp2(st - m_new).astype(BF16)
            acc_sc[...] = acc_sc[...] * jnp.exp2(m_prev - m_new) + jnp.dot(vt_sc[j], p, preferred_element_type=F32)
            m_sc[...] = m_new
            return carry

        lax.fori_loop(0, nk, step, 0)

    ot = acc_sc[:dh, :] / acc_sc[dh:dh + 1, :]
    for g in range(groups):
        o_ref[:, g * dh:(g + 1) * dh] = ot[:, g * tq:(g + 1) * tq].T.astype(o_ref.dtype)


def attention(qk, vg, n_heads):
    B, S, _ = qk.shape
    dh = ATTN_HEAD_DIM
    kvh = ATTN_KV_HEADS
    groups = n_heads // kvh
    tq = _pick(S, ATTN_TQ, 8)
    tk = _pick(S, ATTN_TK, 8)
    gw = groups * dh
    return pl.pallas_call(
        functools.partial(_attn_body, tk=tk, groups=groups),
        grid=(B, kvh, S // tq),
        in_specs=[pl.BlockSpec((None, tq, gw), lambda b, k, i: (b, i, k)),
                  pl.BlockSpec((None, S, dh), lambda b, k, i: (b, 0, n_heads + k)),
                  pl.BlockSpec((None, S, dh), lambda b, k, i: (b, 0, k))],
        out_specs=pl.BlockSpec((None, tq, gw), lambda b, k, i: (b, i, k)),
        out_shape=jax.ShapeDtypeStruct((B, S, n_heads * dh), BF16),
        scratch_shapes=[pltpu.VMEM((S // tk, dh + ONES_ROWS, tk), BF16),
                        pltpu.VMEM((dh, groups * tq), BF16),
                        pltpu.VMEM((1, LANES), F32),
                        pltpu.VMEM((1, groups * tq), F32),
                        pltpu.VMEM((dh + ONES_ROWS, groups * tq), F32)],
        compiler_params=_params("arbitrary", "arbitrary", "arbitrary"),
        name="attention",
    )(qk, qk, vg)


RET_UNROLL = 8


def _ret_body(dec_ref, q_ref, k_ref, v_ref, gr_ref, g_ref, o_ref, rf_sc, rb_sc, racc_sc, *, chunk):
    C = chunk
    S, dk = q_ref.shape
    n = S // C
    h = pl.program_id(1)
    lgf = -jnp.exp(jnp.full((1, 1), dec_ref[0, h], F32))
    lgb = -jnp.exp(jnp.full((1, 1), dec_ref[1, h], F32))
    idx = lax.broadcasted_iota(jnp.int32, (C, 1), 0).astype(F32)
    zeta_f = jnp.exp(lgf * (C - 1.0 - idx))
    xi_f = jnp.exp(lgf * (idx + 1.0))
    zeta_b = jnp.exp(lgb * idx)
    xi_b = jnp.exp(lgb * (C - idx))
    cd_f = jnp.exp(lgf * C)
    cd_b = jnp.exp(lgb * C)
    diff = (lax.broadcasted_iota(jnp.int32, (C, C), 0) - lax.broadcasted_iota(jnp.int32, (C, C), 1)).astype(F32)
    decay = jnp.where(diff >= 0, jnp.exp(lgf * jnp.maximum(diff, 0.0)), jnp.exp(lgb * jnp.maximum(-diff, 0.0)))

    def kv_update(i, zeta, cd):
        off = pl.multiple_of(i * C, C)
        kz = (k_ref[pl.ds(off, C), :].astype(F32) * zeta).astype(BF16)
        kv = lax.dot_general(kz, v_ref[pl.ds(off, C), :], TN_DIMS, preferred_element_type=F32)
        racc_sc[...] = racc_sc[...] * cd + kv

    racc_sc[...] = jnp.zeros(racc_sc.shape, F32)

    def fstep(i, carry):
        rf_sc[i] = racc_sc[...].astype(BF16)
        kv_update(i, zeta_f, cd_f)
        return carry

    lax.fori_loop(0, n, fstep, 0, unroll=RET_UNROLL)
    racc_sc[...] = jnp.zeros(racc_sc.shape, F32)

    def bstep(t, carry):
        i = n - 1 - t
        rb_sc[i] = racc_sc[...].astype(BF16)
        kv_update(i, zeta_b, cd_b)
        return carry

    lax.fori_loop(0, n, bstep, 0, unroll=RET_UNROLL)
    gain = g_ref[...]

    def ostep(i, carry):
        off = pl.multiple_of(i * C, C)
        qi = q_ref[pl.ds(off, C), :]
        ki = k_ref[pl.ds(off, C), :]
        vi = v_ref[pl.ds(off, C), :]
        s = lax.dot_general(qi, ki, NT_DIMS, preferred_element_type=F32) * decay
        qf = (qi.astype(F32) * xi_f).astype(BF16)
        qb = (qi.astype(F32) * xi_b).astype(BF16)
        o = (jnp.dot(s.astype(BF16), vi, preferred_element_type=F32)
             + jnp.dot(qf, rf_sc[i], preferred_element_type=F32)
             + jnp.dot(qb, rb_sc[i], preferred_element_type=F32))
        mu = jnp.mean(o, axis=-1, keepdims=True)
        d = o - mu
        var = jnp.mean(d * d, axis=-1, keepdims=True)
        y = d * lax.rsqrt(var + NORM_EPS) * gain
        o_ref[pl.ds(off, C), :] = (y * _silu(gr_ref[pl.ds(off, C), :].astype(F32))).astype(o_ref.dtype)
        return carry

    lax.fori_loop(0, n, ostep, 0, unroll=RET_UNROLL)


def retention(qk, vg, decays, gain, n_heads, v_off, g_off):
    B, S, _ = qk.shape
    d = RET_HEAD_DIM
    n = S // RET_CHUNK
    blk = lambda off: pl.BlockSpec((None, S, d), lambda b, h: (b, 0, off + h))
    return pl.pallas_call(
        functools.partial(_ret_body, chunk=RET_CHUNK),
        grid=(B, n_heads),
        in_specs=[pl.BlockSpec(memory_space=pltpu.SMEM),
                  blk(0), blk(n_heads), blk(v_off), blk(g_off),
                  pl.BlockSpec((None, 1, d), lambda b, h: (h, 0, 0))],
        out_specs=blk(0),
        out_shape=jax.ShapeDtypeStruct((B, S, n_heads * d), BF16),
        scratch_shapes=[pltpu.VMEM((n, d, d), BF16), pltpu.VMEM((n, d, d), BF16), pltpu.VMEM((d, d), F32)],
        compiler_params=_params("parallel", "parallel"),
        name="retention",
    )(decays, qk, qk, vg, vg, gain.reshape(n_heads, 1, d))


def _outproj_body(a_ref, r_ref, wa_ref, wr_ref, x_ref, mod_ref, o_ref, wb_sc, *, gate_row):
    _stage_weights([wa_ref, wr_ref], None, wb_sc)
    ka = a_ref.shape[1]
    acc = (jnp.dot(a_ref[...], wb_sc[:ka, :], preferred_element_type=F32)
           + jnp.dot(r_ref[...], wb_sc[ka:, :], preferred_element_type=F32))
    o_ref[...] = x_ref[...] + mod_ref[gate_row:gate_row + 1, :] * acc


def outproj(attn_out, ret_out, w_o, x2d, mod3, S, gate_row):
    T, Ka = attn_out.shape
    Kr = ret_out.shape[1]
    assert Ka == Kr and w_o.shape[0] == Ka + Kr
    N = w_o.shape[1]
    tm = _pick(S, 1024, 8)
    tn = _pick(N, 512, 2 * LANES)
    per_b = S // tm
    return pl.pallas_call(
        functools.partial(_outproj_body, gate_row=gate_row),
        grid=(N // tn, T // tm),
        in_specs=[pl.BlockSpec((tm, Ka), lambda j, i: (i, 0)),
                  pl.BlockSpec((tm, Kr), lambda j, i: (i, 0)),
                  pl.BlockSpec((Ka, tn), lambda j, i: (0, j)),
                  pl.BlockSpec((Kr, tn), lambda j, i: (1, j)),
                  pl.BlockSpec((tm, tn), lambda j, i: (i, j)),
                  pl.BlockSpec((None, 6, tn), lambda j, i: (i // per_b, 0, j))],
        out_specs=pl.BlockSpec((tm, tn), lambda j, i: (i, j)),
        out_shape=jax.ShapeDtypeStruct((T, N), F32),
        scratch_shapes=[pltpu.VMEM((Ka + Kr, tn), BF16)],
        compiler_params=_params("arbitrary", "arbitrary"),
        name="outproj",
    )(attn_out, ret_out, w_o, w_o, x2d, mod3)


def _router_body(x_ref, g_ref, mod_ref, wr_ref, h_ref, aff_ref, *, shift_row, scale_row):
    h = _modulated_norm(x_ref[...], g_ref[...], mod_ref[shift_row:shift_row + 1, :],
                        mod_ref[scale_row:scale_row + 1, :])
    h_ref[...] = h
    hi = h.astype(BF16)
    lo = (h - hi.astype(F32)).astype(BF16)
    whi = wr_ref[0]
    wlo = wr_ref[1]
    lg = (lax.dot_general(whi, hi, NT_DIMS, preferred_element_type=F32)
          + lax.dot_general(whi, lo, NT_DIMS, preferred_element_type=F32)
          + lax.dot_general(wlo, hi, NT_DIMS, preferred_element_type=F32))
    e = jnp.exp(lg - jnp.max(lg, axis=0, keepdims=True))
    aff_ref[...] = e / jnp.sum(e, axis=0, keepdims=True)


def router(x1, g, mod3, w_router, shift_row, scale_row):
    B, S, D = x1.shape
    E = w_router.shape[1]
    wt = w_router.T
    whi = wt.astype(BF16)
    wsplit = jnp.stack([whi, (wt - whi.astype(F32)).astype(BF16)])
    ts = _pick(S, 512)
    return pl.pallas_call(
        functools.partial(_router_body, shift_row=shift_row, scale_row=scale_row),
        grid=(B, S // ts),
        in_specs=[pl.BlockSpec((None, ts, D), lambda b, i: (b, i, 0)),
                  pl.BlockSpec((1, D), lambda b, i: (0, 0)),
                  pl.BlockSpec((None, 6, D), lambda b, i: (b, 0, 0)),
                  pl.BlockSpec((2, E, D), lambda b, i: (0, 0, 0))],
        out_specs=[pl.BlockSpec((None, ts, D), lambda b, i: (b, i, 0)),
                   pl.BlockSpec((None, E, ts), lambda b, i: (b, 0, i))],
        out_shape=[jax.ShapeDtypeStruct((B, S, D), F32), jax.ShapeDtypeStruct((B, E, S), F32)],
        compiler_params=_params("parallel", "parallel"),
        name="router",
    )(x1, g.reshape(1, D), mod3, wsplit)


def _cumsum_lanes(x):
    rows, S = x.shape
    tri = (lax.broadcasted_iota(jnp.int32, (LANES, LANES), 0)
           <= lax.broadcasted_iota(jnp.int32, (LANES, LANES), 1))
    tri = jnp.where(tri, 1.0, 0.0).astype(BF16)
    carry = jnp.zeros((rows, 1), F32)
    outs = []
    for c in range(S // LANES):
        inc = jnp.dot(x[:, c * LANES:(c + 1) * LANES].astype(BF16), tri, preferred_element_type=F32) + carry
        outs.append(inc)
        carry = inc[:, LANES - 1:LANES]
    return jnp.concatenate(outs, axis=1)


COMBINE_TM = 128
SLOT_CHUNK = 16
KBLOCK = 256


def _select_body(aff_ref, pos_ref, idx_ref, ts_ref, tc_ref, *, cap, tm):
    aff = aff_ref[...]
    E, S = aff.shape

    def count(mask):
        return jnp.sum(jnp.where(mask, 1.0, 0.0), axis=1, keepdims=True)

    def search(i, t):
        cand = t | jnp.left_shift(jnp.int32(1), 30 - i)
        return jnp.where(count(aff >= pltpu.bitcast(cand, F32)) >= cap, cand, t)

    thr = pltpu.bitcast(lax.fori_loop(0, 31, search, jnp.zeros((E, 1), jnp.int32)), F32)
    gt = aff > thr
    eq = aff == thr
    need = cap - count(gt)
    eq_rank = _cumsum_lanes(jnp.where(eq, 1.0, 0.0))
    sel = gt | (eq & (eq_rank <= need))
    rank = _cumsum_lanes(jnp.where(sel, 1.0, 0.0))
    sel = sel & (rank <= cap)
    sel_f = jnp.where(sel, 1.0, 0.0)
    pos = jnp.where(sel, rank - 1.0, -1.0).astype(jnp.int32)
    pos_ref[...] = pos

    tile_of = lax.broadcasted_iota(jnp.int32, (S, LANES), 0) // tm
    lane = lax.broadcasted_iota(jnp.int32, (S, LANES), 1)
    cnt = jnp.dot(sel_f.astype(BF16), jnp.where(tile_of == lane, 1.0, 0.0).astype(BF16),
                  preferred_element_type=F32)
    r = lax.broadcasted_iota(jnp.int32, (LANES, LANES), 0)
    c = lax.broadcasted_iota(jnp.int32, (LANES, LANES), 1)
    start = jnp.dot(cnt.astype(BF16), jnp.where(r < c, 1.0, 0.0).astype(BF16), preferred_element_type=F32)
    tc_ref[...] = cnt.astype(jnp.int32)
    ts_ref[...] = start.astype(jnp.int32)

    slot = lax.broadcasted_iota(jnp.int32, (cap, S), 0)
    tok = (lax.broadcasted_iota(jnp.int32, (1, S), 1) + pl.program_id(0) * S).astype(F32)
    for e in range(E):
        col = jnp.sum(jnp.where(pos[e:e + 1, :] == slot, tok, 0.0), axis=1, keepdims=True)
        idx_ref[e:e + 1, :] = jnp.broadcast_to(col, (cap, LANES)).T[0:1, :].astype(jnp.int32)


def select(aff_t, cap, tm):
    B, E, S = aff_t.shape
    assert S // tm <= LANES and tm <= 256
    blk = lambda n: pl.BlockSpec((None, E, n), lambda b: (b, 0, 0))
    return pl.pallas_call(
        functools.partial(_select_body, cap=cap, tm=tm),
        grid=(B,),
        in_specs=[blk(S)],
        out_specs=[blk(S), blk(cap), blk(LANES), blk(LANES)],
        out_shape=[jax.ShapeDtypeStruct((B, E, S), jnp.int32), jax.ShapeDtypeStruct((B, E, cap), jnp.int32),
                   jax.ShapeDtypeStruct((B, E, LANES), jnp.int32), jax.ShapeDtypeStruct((B, E, LANES), jnp.int32)],
        compiler_params=_params("parallel"),
        name="select",
    )(aff_t)


def _gather_rows_body(idx_ref, h_hbm, xe_ref, stage, sems, *, cap):
    g = pl.program_id(0)
    slot = g % 2

    def row_copy(slot, p, row):
        return pltpu.make_async_copy(h_hbm.at[pl.ds(row, 1), :], stage.at[slot, pl.ds(p, 1), :], sems.at[slot])

    def issue_group(grp, slot):
        base = grp * cap
        for p in range(cap):
            row_copy(slot, p, idx_ref[base + p]).start()

    @pl.when(g == 0)
    def _():
        issue_group(0, 0)

    @pl.when(g + 1 < pl.num_programs(0))
    def _():
        issue_group(g + 1, 1 - slot)

    def drain(p, carry):
        row_copy(slot, p, 0).wait()
        return carry

    lax.fori_loop(0, cap, drain, 0, unroll=8)
    xe_ref[...] = stage[slot].astype(xe_ref.dtype)


def gather_rows(idx, h2, cap):
    B, E, _ = idx.shape
    D = h2.shape[2]
    h2 = h2.reshape(-1, D)
    return pl.pallas_call(
        functools.partial(_gather_rows_body, cap=cap),
        grid_spec=pltpu.PrefetchScalarGridSpec(
            num_scalar_prefetch=1,
            grid=(B * E,),
            in_specs=[pl.BlockSpec(memory_space=pl.ANY)],
            out_specs=pl.BlockSpec((None, cap, D), lambda g, idx: (g % E, g // E, 0)),
            scratch_shapes=[pltpu.VMEM((2, cap, D), F32), pltpu.SemaphoreType.DMA((2,))]),
        out_shape=jax.ShapeDtypeStruct((E, B * cap, D), BF16),
        compiler_params=_params("arbitrary"),
        name="gather_rows",
    )(idx.reshape(-1), h2)


def _up_body(x_ref, w1_ref, w3_ref, o_ref):
    x = x_ref[...]
    a = jnp.dot(x, w1_ref[...].astype(BF16), preferred_element_type=F32)
    b = jnp.dot(x, w3_ref[...].astype(BF16), preferred_element_type=F32)
    o_ref[...] = (_silu(a) * b).astype(o_ref.dtype)


def expert_up(xe, w1, w3):
    E, M, D = xe.shape
    Fd = w1.shape[2]
    tm = _pick(M, 1024, 8)
    tf = _pick(Fd, 256, 2 * LANES)
    return pl.pallas_call(
        _up_body,
        grid=(E, M // tm, Fd // tf),
        in_specs=[pl.BlockSpec((None, tm, D), lambda e, m, f: (e, m, 0)),
                  pl.BlockSpec((None, D, tf), lambda e, m, f: (e, 0, f)),
                  pl.BlockSpec((None, D, tf), lambda e, m, f: (e, 0, f))],
        out_specs=pl.BlockSpec((None, tm, tf), lambda e, m, f: (e, m, f)),
        out_shape=jax.ShapeDtypeStruct((E, M, Fd), BF16),
        compiler_params=_params("parallel", "parallel", "arbitrary"),
        name="expert_up",
    )(xe, w1, w3)


def _down_body(h_ref, w2_ref, o_ref):
    o_ref[...] = jnp.dot(h_ref[...], w2_ref[...].astype(BF16), preferred_element_type=F32).astype(o_ref.dtype)


def expert_down(hid, w2):
    E, M, Fd = hid.shape
    D = w2.shape[2]
    tm = _pick(M, 2048, 8)
    tn = _pick(D, 512, 2 * LANES)
    return pl.pallas_call(
        _down_body,
        grid=(E, M // tm, D // tn),
        in_specs=[pl.BlockSpec((None, tm, Fd), lambda e, m, n: (e, m, 0)),
                  pl.BlockSpec((None, Fd, tn), lambda e, m, n: (e, 0, n))],
        out_specs=pl.BlockSpec((None, tm, tn), lambda e, m, n: (e, m, n)),
        out_shape=jax.ShapeDtypeStruct((E, M, D), BF16),
        compiler_params=_params("parallel", "parallel", "arbitrary"),
        name="expert_down",
    )(hid, w2)


def _combine_body(ts_ref, tc_ref, pos_ref, aff_ref, ye_hbm, x_ref, mod_ref, g_ref, o_ref,
                  buf, wt, acc_sc, sems, *, cap, gate_row):
    b = pl.program_id(0)
    t = pl.program_id(1)
    n_b = pl.num_programs(0)
    n_t = pl.num_programs(1)
    E, tm = pos_ref.shape
    nt_pad = ts_ref.shape[0] // (n_b * E)
    C = SLOT_CHUNK
    shift = C.bit_length() - 1
    step = b * n_t + t
    slot = step % 2

    def chunk_copy(e, row0, slot_, k):
        return pltpu.make_async_copy(ye_hbm.at[e, pl.ds(row0, C), :], buf.at[slot_, pl.ds(k, C), :], sems.at[slot_])

    def for_each_chunk(bb, tt, fn):
        k = jnp.int32(0)
        for e in range(E):
            base = (bb * E + e) * nt_pad + tt
            first = ts_ref[base]
            n = tc_ref[base]
            first_al = lax.shift_left(lax.shift_right_logical(first, shift), shift)
            n_chunks = jnp.where(n > 0, lax.shift_right_logical(first + n - first_al + (C - 1), shift), 0)

            def chunk(c, k, e=e, first_al=first_al):
                k = pl.multiple_of(k, C)
                fn(e, first_al + c * C, k)
                return k + C

            k = lax.fori_loop(0, n_chunks, chunk, k)
        return k

    def start_tile(bb, tt, slot_):
        for_each_chunk(bb, tt, lambda e, p0, k: chunk_copy(e, pl.multiple_of(bb * cap + p0, C), slot_, k).start())

    @pl.when(step == 0)
    def _():
        buf[...] = jnp.zeros(buf.shape, buf.dtype)
        start_tile(0, 0, 0)

    @pl.when(step + 1 < n_b * n_t)
    def _():
        wrap = t + 1 == n_t
        start_tile(jnp.where(wrap, b + 1, b), jnp.where(wrap, 0, t + 1), 1 - slot)

    def weights(e, p0, k):
        rows = p0 + lax.broadcasted_iota(jnp.int32, (C, tm), 0)
        wt[pl.ds(k, C), :] = jnp.where(pos_ref[e:e + 1, :] == rows, aff_ref[e:e + 1, :], 0.0).astype(BF16)

    k = for_each_chunk(b, t, weights)
    n_used = lax.shift_right_logical(k, shift)
    n_blocks = lax.shift_right_logical(k + (KBLOCK - 1), KBLOCK.bit_length() - 1)

    def zero_tail(c, carry):
        wt[pl.ds(pl.multiple_of(c * C, C), C), :] = jnp.zeros((C, tm), BF16)
        return carry

    lax.fori_loop(n_used, n_blocks * (KBLOCK // C), zero_tail, 0)

    def drain(c, carry):
        chunk_copy(0, 0, slot, 0).wait()
        return carry

    lax.fori_loop(0, n_used, drain, 0)
    acc_sc[...] = jnp.zeros(acc_sc.shape, F32)

    def block(kb, carry):
        r0 = pl.multiple_of(kb * KBLOCK, KBLOCK)
        acc_sc[...] += lax.dot_general(wt[pl.ds(r0, KBLOCK), :], buf[slot, pl.ds(r0, KBLOCK), :], TN_DIMS,
                                       preferred_element_type=F32)
        return carry

    lax.fori_loop(0, n_blocks, block, 0)
    x2 = x_ref[...] + mod_ref[gate_row:gate_row + 1, :] * acc_sc[...]
    ms = jnp.mean(x2 * x2, axis=-1, keepdims=True)
    o_ref[...] = x2 * lax.rsqrt(ms + NORM_EPS) * g_ref[...]


def combine(tstart, tcount, pos, aff_t, ye, x1, mod3, final_g, cap, gate_row):
    B, E, S = pos.shape
    D = x1.shape[2]
    tm = min(COMBINE_TM, S)
    kcap = -(-E * (tm + SLOT_CHUNK) // KBLOCK) * KBLOCK
    return pl.pallas_call(
        functools.partial(_combine_body, cap=cap, gate_row=gate_row),
        grid_spec=pltpu.PrefetchScalarGridSpec(
            num_scalar_prefetch=2,
            grid=(B, S // tm),
            in_specs=[pl.BlockSpec((None, E, tm), lambda b, t, *_: (b, 0, t)),
                      pl.BlockSpec((None, E, tm), lambda b, t, *_: (b, 0, t)),
                      pl.BlockSpec(memory_space=pl.ANY),
                      pl.BlockSpec((None, tm, D), lambda b, t, *_: (b, t, 0)),
                      pl.BlockSpec((None, 6, D), lambda b, t, *_: (b, 0, 0)),
                      pl.BlockSpec((1, D), lambda b, t, *_: (0, 0))],
            out_specs=pl.BlockSpec((None, tm, D), lambda b, t, *_: (b, t, 0)),
            scratch_shapes=[pltpu.VMEM((2, kcap, D), BF16), pltpu.VMEM((kcap, tm), BF16),
                            pltpu.VMEM((tm, D), F32), pltpu.SemaphoreType.DMA((2,))]),
        out_shape=jax.ShapeDtypeStruct((B, S, D), F32),
        compiler_params=_params("arbitrary", "arbitrary"),
        name="combine",
    )(tstart.reshape(-1), tcount.reshape(-1), pos, aff_t, ye, x1, mod3, final_g.reshape(1, D))


def kernel(x, c, w_ada, b_ada, norm1_g, w_in, attn_q_norm_g, attn_k_norm_g, ret_decay_fwd, ret_decay_bwd,
           ret_norm_g, w_out, norm2_g, w_router, w1, w3, w2, final_g):
    B, S, D = x.shape
    T = B * S
    mix = w_out.shape[0]
    attn_w = mix // 2
    ret_w = mix - attn_w
    ah = attn_w // ATTN_HEAD_DIM
    kv_w = ATTN_KV_HEADS * ATTN_HEAD_DIM
    rh = ret_w // RET_HEAD_DIM
    E = w_router.shape[1]
    cap = CAPACITY_FACTOR * S // E

    o1 = attn_w + kv_w
    o2 = o1 + kv_w
    o3 = o2 + 2 * ret_w
    gq = _deinterleave_vec(attn_q_norm_g, ATTN_HEAD_DIM)
    gk = _deinterleave_vec(attn_k_norm_g, ATTN_HEAD_DIM)
    gain_a = jnp.concatenate([jnp.tile(gq, ah), jnp.tile(gk, ATTN_KV_HEADS)]).reshape(1, o1)
    q_scale = math.log2(math.e) / math.sqrt(ATTN_HEAD_DIM)
    cs_a = jnp.concatenate([jnp.full((attn_w,), q_scale, F32), jnp.ones((kv_w,), F32)]).reshape(1, o1)
    cs_b = jnp.concatenate([jnp.ones((ret_w,), F32), jnp.full((ret_w,), RET_HEAD_DIM ** -0.5, F32)]).reshape(1, 2 * ret_w)
    cos_a, sin_a = _rope_tables(S, ATTN_HEAD_DIM)
    cos_a2 = jnp.asarray(np.concatenate([cos_a, cos_a], axis=1))
    sin_a2 = jnp.asarray(np.concatenate([-sin_a, sin_a], axis=1))
    cos_r, sin_r = (jnp.asarray(t) for t in _rope_tables(S, RET_HEAD_DIM))

    mod3 = adaln(c, w_ada, b_ada).reshape(B, 6, D)

    h = modulate(x, norm1_g, mod3, 0, 1).reshape(T, D)
    tn = _pick(kv_w, 512, PERM_BLOCK)
    head_of = np.arange(tn) // ATTN_HEAD_DIM
    head_ones = jnp.asarray(head_of[:, None] == head_of[None, :], BF16)
    qk_a = _proj_call(_proj_attn_body, h, w_in, o1, tn, lambda j: j,
                      [_deinterleave_matrix(ATTN_HEAD_DIM), head_ones], [gain_a, cs_a], [cos_a2, sin_a2], S, "proj_attn_qk")
    qk_r = _proj_call(_proj_ret_body, h, w_in, 2 * ret_w, tn, lambda j: o2 // tn + j,
                      [_deinterleave_matrix(RET_HEAD_DIM)], [cs_b], [cos_r, sin_r], S, "proj_ret_qk")
    vg = _proj_call(_proj_plain_body, h, w_in, kv_w + 2 * ret_w, tn,
                    lambda j: jnp.where(j < kv_w // tn, o1 // tn + j, (o3 - kv_w) // tn + j), [], [], [], S, "proj_vg")
    attn_out = attention(qk_a.reshape(B, S, o1), vg.reshape(B, S, -1), ah)
    decays = jnp.stack([ret_decay_fwd, ret_decay_bwd]).astype(F32)
    v_off = kv_w // RET_HEAD_DIM
    ret_out = retention(qk_r.reshape(B, S, 2 * ret_w), vg.reshape(B, S, -1), decays, ret_norm_g.astype(F32),
                        rh, v_off, v_off + rh)
    x1 = outproj(attn_out.reshape(T, attn_w), ret_out.reshape(T, ret_w), w_out,
                 x.reshape(T, D), mod3, S, 2).reshape(B, S, D)

    h2, aff_t = router(x1, norm2_g, mod3, w_router, 3, 4)
    pos, idx, tstart, tcount = select(aff_t, cap, min(COMBINE_TM, S))
    xe = gather_rows(idx, h2, cap)
    ye = expert_down(expert_up(xe, w1, w3), w2)
    return combine(tstart, tcount, pos, aff_t, ye, x1, mod3, final_g, cap, 5)
```

```python
import functools
import math

import numpy as np
import jax
import jax.numpy as jnp
from jax import lax
from jax.experimental import pallas as pl
from jax.experimental.pallas import tpu as pltpu

F32 = jnp.float32
BF16 = jnp.bfloat16

GRID_W = 64
ROPE_THETA = 10000.0
NORM_EPS = 1e-6
ATTN_HEAD_DIM = 128
ATTN_KV_HEADS = 4
RET_HEAD_DIM = 256
RET_CHUNK = 128
CAPACITY_FACTOR = 2

LANES = 128
VMEM_LIMIT_BYTES = 56 * 1024 * 1024

NT_DIMS = (((1,), (1,)), ((), ()))
TN_DIMS = (((0,), (0,)), ((), ()))


def _params(*sem):
    return pltpu.CompilerParams(dimension_semantics=sem, vmem_limit_bytes=VMEM_LIMIT_BYTES)


def _pick(dim, pref, align=LANES):
    if dim <= pref:
        return dim
    t = (pref // align) * align
    while t >= align:
        if dim % t == 0:
            return t
        t -= align
    return dim


def _silu(v):
    return v * jax.nn.sigmoid(v)


def _adaln_body(c_ref, w_ref, b_ref, o_ref):
    c = c_ref[...]
    bp = c.shape[0]
    sc = _silu(c)
    hi = sc.astype(BF16).astype(F32)
    lhs = jnp.concatenate([hi, sc - hi], axis=0).astype(BF16)
    acc = jnp.dot(lhs, w_ref[...].astype(BF16), preferred_element_type=F32)
    o_ref[...] = acc[:bp] + acc[bp:] + b_ref[...]


def adaln(c, w_ada, b_ada):
    B, D = c.shape
    N = w_ada.shape[1]
    bp = -(-B // 8) * 8
    cp = jnp.pad(c, ((0, bp - B), (0, 0)))
    tn = _pick(N, 512)
    out = pl.pallas_call(
        _adaln_body,
        grid=(N // tn,),
        in_specs=[pl.BlockSpec((bp, D), lambda j: (0, 0)),
                  pl.BlockSpec((D, tn), lambda j: (0, j)),
                  pl.BlockSpec((1, tn), lambda j: (0, j))],
        out_specs=pl.BlockSpec((bp, tn), lambda j: (0, j)),
        out_shape=jax.ShapeDtypeStruct((bp, N), F32),
        compiler_params=_params("arbitrary"),
        name="adaln",
    )(cp, w_ada, b_ada.reshape(1, N))
    return out[:B]


def _modulated_norm(x, g, shift, scale):
    ms = jnp.mean(x * x, axis=-1, keepdims=True)
    return x * lax.rsqrt(ms + NORM_EPS) * g * (1.0 + scale) + shift


def _modulate_body(x_ref, g_ref, mod_ref, o_ref, *, shift_row, scale_row):
    h = _modulated_norm(x_ref[...], g_ref[...], mod_ref[shift_row:shift_row + 1, :],
                        mod_ref[scale_row:scale_row + 1, :])
    o_ref[...] = h.astype(o_ref.dtype)


def modulate(x, g, mod3, shift_row, scale_row):
    B, S, D = x.shape
    ts = _pick(S, 512, 8)
    return pl.pallas_call(
        functools.partial(_modulate_body, shift_row=shift_row, scale_row=scale_row),
        grid=(B, S // ts),
        in_specs=[pl.BlockSpec((None, ts, D), lambda b, i: (b, i, 0)),
                  pl.BlockSpec((1, D), lambda b, i: (0, 0)),
                  pl.BlockSpec((None, 6, D), lambda b, i: (b, 0, 0))],
        out_specs=pl.BlockSpec((None, ts, D), lambda b, i: (b, i, 0)),
        out_shape=jax.ShapeDtypeStruct((B, S, D), BF16),
        compiler_params=_params("parallel", "parallel"),
        name="modulate",
    )(x, g.reshape(1, D), mod3)


PERM_BLOCK = 256


def _stage_weights(w_refs, perm_ref, wb_sc):
    @pl.when(pl.program_id(1) == 0)
    def _():
        r0 = 0
        for w_ref in w_refs:
            rows = slice(r0, r0 + w_ref.shape[0])
            r0 += w_ref.shape[0]
            if perm_ref is None:
                wb_sc[rows, :] = w_ref[...].astype(BF16)
            else:
                for c in range(w_ref.shape[1] // PERM_BLOCK):
                    cols = slice(c * PERM_BLOCK, (c + 1) * PERM_BLOCK)
                    wb_sc[rows, cols] = jnp.dot(w_ref[:, cols].astype(BF16), perm_ref[...],
                                                preferred_element_type=F32).astype(BF16)


def _proj_attn_body(h_ref, w_ref, perm_ref, ones_ref, g_ref, cs_ref, cos_ref, sin_ref, o_ref, wb_sc):
    _stage_weights([w_ref], perm_ref, wb_sc)
    acc = jnp.dot(h_ref[...], wb_sc[...], preferred_element_type=F32)
    d = ATTN_HEAD_DIM
    ssq = jnp.dot((acc * acc).astype(BF16), ones_ref[...], preferred_element_type=F32)
    y = acc * lax.rsqrt(ssq * (1.0 / d) + NORM_EPS) * g_ref[...]
    cosv = cos_ref[...]
    sinv = sin_ref[...]
    for hh in range(acc.shape[1] // d):
        sl = slice(hh * d, (hh + 1) * d)
        yh = y[:, sl]
        rot = pltpu.roll(yh, d // 2, axis=1)
        o_ref[:, sl] = ((yh * cosv + rot * sinv) * cs_ref[:, sl]).astype(o_ref.dtype)


def _proj_ret_body(h_ref, w_ref, perm_ref, cs_ref, cos_ref, sin_ref, o_ref, wb_sc):
    _stage_weights([w_ref], perm_ref, wb_sc)
    acc = jnp.dot(h_ref[...], wb_sc[...], preferred_element_type=F32)
    cosv = cos_ref[...]
    sinv = sin_ref[...]
    d = RET_HEAD_DIM
    for hh in range(acc.shape[1] // d):
        s1 = slice(hh * d, hh * d + d // 2)
        s2 = slice(hh * d + d // 2, (hh + 1) * d)
        x1 = acc[:, s1]
        x2 = acc[:, s2]
        o_ref[:, s1] = ((x1 * cosv - x2 * sinv) * cs_ref[:, s1]).astype(o_ref.dtype)
        o_ref[:, s2] = ((x1 * sinv + x2 * cosv) * cs_ref[:, s2]).astype(o_ref.dtype)


def _proj_plain_body(h_ref, w_ref, o_ref, wb_sc):
    _stage_weights([w_ref], None, wb_sc)
    o_ref[...] = jnp.dot(h_ref[...], wb_sc[...], preferred_element_type=F32).astype(o_ref.dtype)


def _proj_call(body, h2d, w, n_out, tn, in_block, mats, rows, tables, S, name):
    T, K = h2d.shape
    tm = _pick(S, 1024, 8)
    per_b = S // tm
    in_specs = [pl.BlockSpec((tm, K), lambda j, i: (i, 0)),
                pl.BlockSpec((K, tn), lambda j, i: (0, in_block(j)))]
    in_specs += [pl.BlockSpec(m.shape, lambda j, i: (0, 0)) for m in mats]
    in_specs += [pl.BlockSpec((1, tn), lambda j, i: (0, j)) for _ in rows]
    in_specs += [pl.BlockSpec((tm, t.shape[1]), lambda j, i: (i % per_b, 0)) for t in tables]
    return pl.pallas_call(
        body,
        grid=(n_out // tn, T // tm),
        in_specs=in_specs,
        out_specs=pl.BlockSpec((tm, tn), lambda j, i: (i, j)),
        out_shape=jax.ShapeDtypeStruct((T, n_out), BF16),
        scratch_shapes=[pltpu.VMEM((K, tn), BF16)],
        compiler_params=_params("arbitrary", "arbitrary"),
        name=name,
    )(h2d, w, *mats, *rows, *tables)


def _rope_tables(S, d):
    quarter = d // 4
    t = np.arange(S)
    inv = ROPE_THETA ** (-np.arange(quarter, dtype=np.float64) / quarter)
    ang = np.concatenate([(t // GRID_W)[:, None] * inv, (t % GRID_W)[:, None] * inv], axis=-1)
    return np.cos(ang).astype(np.float32), np.sin(ang).astype(np.float32)


def _deinterleave_vec(g, d):
    return g.reshape(-1, d // 2, 2).transpose(0, 2, 1).reshape(-1)


def _deinterleave_matrix(d):
    src = np.concatenate([h * d + np.concatenate([np.arange(0, d, 2), np.arange(1, d, 2)])
                          for h in range(PERM_BLOCK // d)])
    p = np.zeros((PERM_BLOCK, PERM_BLOCK), np.float32)
    p[src, np.arange(PERM_BLOCK)] = 1.0
    return jnp.asarray(p, BF16)


ATTN_TQ = 256
ATTN_SUB = 2
ATTN_TK = 512
ONES_ROWS = 16
ATTN_SAFE_EXP = 50.0


def _attn_body(q_ref, k_ref, v_ref, o_ref, vt_sc, qt_sc, kmax_sc, m_sc, acc_sc, *, tq, tk, groups):
    S, dh = k_ref.shape
    nk = S // tk

    @pl.when(pl.program_id(2) == 0)
    def _():
        for j in range(nk):
            vt_sc[j, :dh, :] = v_ref[j * tk:(j + 1) * tk, :].astype(F32).T.astype(BF16)
            vt_sc[j, dh:, :] = jnp.ones((ONES_ROWS, tk), BF16)
        kf = k_ref[...].astype(F32)
        ksq = jnp.max(jnp.sum(kf * kf, axis=1, keepdims=True), axis=0, keepdims=True)
        kmax_sc[...] = jnp.broadcast_to(jnp.sqrt(ksq), kmax_sc.shape)

    for r0 in range(0, q_ref.shape[0], tq):
        _attn_queries(q_ref.at[r0:r0 + tq, :], k_ref, o_ref.at[r0:r0 + tq, :], vt_sc, qt_sc, kmax_sc, m_sc, acc_sc,
                      tk=tk, groups=groups)


def _attn_queries(q_ref, k_ref, o_ref, vt_sc, qt_sc, kmax_sc, m_sc, acc_sc, *, tk, groups):
    tq = q_ref.shape[0]
    S, dh = k_ref.shape
    nk = S // tk
    for g in range(groups):
        qt = q_ref[:, g * dh:(g + 1) * dh].astype(F32).T
        qt_sc[:, g * tq:(g + 1) * tq] = qt.astype(BF16)
        m_sc[:, g * tq:(g + 1) * tq] = jnp.sqrt(jnp.sum(qt * qt, axis=0, keepdims=True)) * kmax_sc[:, 0:1]
    bounded = jnp.max(m_sc[...]) <= ATTN_SAFE_EXP

    @pl.when(bounded)
    def _():
        for j in range(nk):
            st = jnp.dot(k_ref[j * tk:(j + 1) * tk, :], qt_sc[...], preferred_element_type=F32)
            p = jnp.exp2(st - m_sc[...]).astype(BF16)
            pv = jnp.dot(vt_sc[j], p, preferred_element_type=F32)
            acc_sc[...] = pv if j == 0 else acc_sc[...] + pv

    @pl.when(jnp.logical_not(bounded))
    def _():
        m_sc[...] = jnp.full(m_sc.shape, -jnp.inf, F32)
        acc_sc[...] = jnp.zeros(acc_sc.shape, F32)

        def step(j, carry):
            off = pl.multiple_of(j * tk, tk)
            st = jnp.dot(k_ref[pl.ds(off, tk), :], qt_sc[...], preferred_element_type=F32)
            m_prev = m_sc[...]
            m_new = jnp.maximum(m_prev, jnp.max(st, axis=0, keepdims=True))
            p = jnp.exp2(st - m_new).astype(BF16)
            acc_sc[...] = acc_sc[...] * jnp.exp2(m_prev - m_new) + jnp.dot(vt_sc[j], p, preferred_element_type=F32)
            m_sc[...] = m_new
            return carry

        lax.fori_loop(0, nk, step, 0)

    ot = acc_sc[:dh, :] / acc_sc[dh:dh + 1, :]
    for g in range(groups):
        o_ref[:, g * dh:(g + 1) * dh] = ot[:, g * tq:(g + 1) * tq].T.astype(o_ref.dtype)


def attention(qk, vg, n_heads):
    B, S, _ = qk.shape
    dh = ATTN_HEAD_DIM
    kvh = ATTN_KV_HEADS
    groups = n_heads // kvh
    tq = _pick(S, ATTN_TQ, 8)
    tk = _pick(S, ATTN_TK, 8)
    tb = _pick(S, ATTN_SUB * tq, tq)
    gw = groups * dh
    return pl.pallas_call(
        functools.partial(_attn_body, tq=tq, tk=tk, groups=groups),
        grid=(B, kvh, S // tb),
        in_specs=[pl.BlockSpec((None, tb, gw), lambda b, k, i: (b, i, k)),
                  pl.BlockSpec((None, S, dh), lambda b, k, i: (b, 0, n_heads + k)),
                  pl.BlockSpec((None, S, dh), lambda b, k, i: (b, 0, k))],
        out_specs=pl.BlockSpec((None, tb, gw), lambda b, k, i: (b, i, k)),
        out_shape=jax.ShapeDtypeStruct((B, S, n_heads * dh), BF16),
        scratch_shapes=[pltpu.VMEM((S // tk, dh + ONES_ROWS, tk), BF16),
                        pltpu.VMEM((dh, groups * tq), BF16),
                        pltpu.VMEM((1, LANES), F32),
                        pltpu.VMEM((1, groups * tq), F32),
                        pltpu.VMEM((dh + ONES_ROWS, groups * tq), F32)],
        compiler_params=_params("arbitrary", "arbitrary", "arbitrary"),
        name="attention",
    )(qk, qk, vg)


RET_UNROLL = 8


def _ret_body(dec_ref, q_ref, k_ref, v_ref, gr_ref, g_ref, o_ref, rf_sc, rb_sc, racc_sc, *, chunk):
    C = chunk
    S, dk = q_ref.shape
    n = S // C
    h = pl.program_id(1)
    lgf = -jnp.exp(jnp.full((1, 1), dec_ref[0, h], F32))
    lgb = -jnp.exp(jnp.full((1, 1), dec_ref[1, h], F32))
    idx = lax.broadcasted_iota(jnp.int32, (C, 1), 0).astype(F32)
    zeta_f = jnp.exp(lgf * (C - 1.0 - idx))
    xi_f = jnp.exp(lgf * (idx + 1.0))
    zeta_b = jnp.exp(lgb * idx)
    xi_b = jnp.exp(lgb * (C - idx))
    cd_f = jnp.exp(lgf * C)
    cd_b = jnp.exp(lgb * C)
    diff = (lax.broadcasted_iota(jnp.int32, (C, C), 0) - lax.broadcasted_iota(jnp.int32, (C, C), 1)).astype(F32)
    decay = jnp.where(diff >= 0, jnp.exp(lgf * jnp.maximum(diff, 0.0)), jnp.exp(lgb * jnp.maximum(-diff, 0.0)))

    def kv_update(i, zeta, cd):
        off = pl.multiple_of(i * C, C)
        kz = (k_ref[pl.ds(off, C), :].astype(F32) * zeta).astype(BF16)
        kv = lax.dot_general(kz, v_ref[pl.ds(off, C), :], TN_DIMS, preferred_element_type=F32)
        racc_sc[...] = racc_sc[...] * cd + kv

    racc_sc[...] = jnp.zeros(racc_sc.shape, F32)

    def fstep(i, carry):
        rf_sc[i] = racc_sc[...].astype(BF16)
        kv_update(i, zeta_f, cd_f)
        return carry

    lax.fori_loop(0, n, fstep, 0, unroll=RET_UNROLL)
    racc_sc[...] = jnp.zeros(racc_sc.shape, F32)

    def bstep(t, carry):
        i = n - 1 - t
        rb_sc[i] = racc_sc[...].astype(BF16)
        kv_update(i, zeta_b, cd_b)
        return carry

    lax.fori_loop(0, n, bstep, 0, unroll=RET_UNROLL)
    gain = g_ref[...]

    def ostep(i, carry):
        off = pl.multiple_of(i * C, C)
        qi = q_ref[pl.ds(off, C), :]
        ki = k_ref[pl.ds(off, C), :]
        vi = v_ref[pl.ds(off, C), :]
        s = lax.dot_general(qi, ki, NT_DIMS, preferred_element_type=F32) * decay
        qf = (qi.astype(F32) * xi_f).astype(BF16)
        qb = (qi.astype(F32) * xi_b).astype(BF16)
        o = (jnp.dot(s.astype(BF16), vi, preferred_element_type=F32)
             + jnp.dot(qf, rf_sc[i], preferred_element_type=F32)
             + jnp.dot(qb, rb_sc[i], preferred_element_type=F32))
        mu = jnp.mean(o, axis=-1, keepdims=True)
        d = o - mu
        var = jnp.mean(d * d, axis=-1, keepdims=True)
        y = d * lax.rsqrt(var + NORM_EPS) * gain
        o_ref[pl.ds(off, C), :] = (y * _silu(gr_ref[pl.ds(off, C), :].astype(F32))).astype(o_ref.dtype)
        return carry

    lax.fori_loop(0, n, ostep, 0, unroll=RET_UNROLL)


def retention(qk, vg, decays, gain, n_heads, v_off, g_off):
    B, S, _ = qk.shape
    d = RET_HEAD_DIM
    n = S // RET_CHUNK
    blk = lambda off: pl.BlockSpec((None, S, d), lambda b, h: (b, 0, off + h))
    return pl.pallas_call(
        functools.partial(_ret_body, chunk=RET_CHUNK),
        grid=(B, n_heads),
        in_specs=[pl.BlockSpec(memory_space=pltpu.SMEM),
                  blk(0), blk(n_heads), blk(v_off), blk(g_off),
                  pl.BlockSpec((None, 1, d), lambda b, h: (h, 0, 0))],
        out_specs=blk(0),
        out_shape=jax.ShapeDtypeStruct((B, S, n_heads * d), BF16),
        scratch_shapes=[pltpu.VMEM((n, d, d), BF16), pltpu.VMEM((n, d, d), BF16), pltpu.VMEM((d, d), F32)],
        compiler_params=_params("parallel", "parallel"),
        name="retention",
    )(decays, qk, qk, vg, vg, gain.reshape(n_heads, 1, d))


def _outproj_body(a_ref, r_ref, wa_ref, wr_ref, x_ref, mod_ref, o_ref, wb_sc, *, gate_row):
    _stage_weights([wa_ref, wr_ref], None, wb_sc)
    ka = a_ref.shape[1]
    acc = (jnp.dot(a_ref[...], wb_sc[:ka, :], preferred_element_type=F32)
           + jnp.dot(r_ref[...], wb_sc[ka:, :], preferred_element_type=F32))
    o_ref[...] = x_ref[...] + mod_ref[gate_row:gate_row + 1, :] * acc


def outproj(attn_out, ret_out, w_o, x2d, mod3, S, gate_row):
    T, Ka = attn_out.shape
    Kr = ret_out.shape[1]
    assert Ka == Kr and w_o.shape[0] == Ka + Kr
    N = w_o.shape[1]
    tm = _pick(S, 1024, 8)
    tn = _pick(N, 512, 2 * LANES)
    per_b = S // tm
    return pl.pallas_call(
        functools.partial(_outproj_body, gate_row=gate_row),
        grid=(N // tn, T // tm),
        in_specs=[pl.BlockSpec((tm, Ka), lambda j, i: (i, 0)),
                  pl.BlockSpec((tm, Kr), lambda j, i: (i, 0)),
                  pl.BlockSpec((Ka, tn), lambda j, i: (0, j)),
                  pl.BlockSpec((Kr, tn), lambda j, i: (1, j)),
                  pl.BlockSpec((tm, tn), lambda j, i: (i, j)),
                  pl.BlockSpec((None, 6, tn), lambda j, i: (i // per_b, 0, j))],
        out_specs=pl.BlockSpec((tm, tn), lambda j, i: (i, j)),
        out_shape=jax.ShapeDtypeStruct((T, N), F32),
        scratch_shapes=[pltpu.VMEM((Ka + Kr, tn), BF16)],
        compiler_params=_params("arbitrary", "arbitrary"),
        name="outproj",
    )(attn_out, ret_out, w_o, w_o, x2d, mod3)


def _router_body(x_ref, g_ref, mod_ref, wr_ref, h_ref, aff_ref, *, shift_row, scale_row):
    h = _modulated_norm(x_ref[...], g_ref[...], mod_ref[shift_row:shift_row + 1, :],
                        mod_ref[scale_row:scale_row + 1, :])
    h_ref[...] = h
    hi = h.astype(BF16)
    lo = (h - hi.astype(F32)).astype(BF16)
    whi = wr_ref[0]
    wlo = wr_ref[1]
    lg = (lax.dot_general(whi, hi, NT_DIMS, preferred_element_type=F32)
          + lax.dot_general(whi, lo, NT_DIMS, preferred_element_type=F32)
          + lax.dot_general(wlo, hi, NT_DIMS, preferred_element_type=F32))
    e = jnp.exp(lg - jnp.max(lg, axis=0, keepdims=True))
    aff_ref[...] = e / jnp.sum(e, axis=0, keepdims=True)


def router(x1, g, mod3, w_router, shift_row, scale_row):
    B, S, D = x1.shape
    E = w_router.shape[1]
    wt = w_router.T
    whi = wt.astype(BF16)
    wsplit = jnp.stack([whi, (wt - whi.astype(F32)).astype(BF16)])
    ts = _pick(S, 512)
    return pl.pallas_call(
        functools.partial(_router_body, shift_row=shift_row, scale_row=scale_row),
        grid=(B, S // ts),
        in_specs=[pl.BlockSpec((None, ts, D), lambda b, i: (b, i, 0)),
                  pl.BlockSpec((1, D), lambda b, i: (0, 0)),
                  pl.BlockSpec((None, 6, D), lambda b, i: (b, 0, 0)),
                  pl.BlockSpec((2, E, D), lambda b, i: (0, 0, 0))],
        out_specs=[pl.BlockSpec((None, ts, D), lambda b, i: (b, i, 0)),
                   pl.BlockSpec((None, E, ts), lambda b, i: (b, 0, i))],
        out_shape=[jax.ShapeDtypeStruct((B, S, D), F32), jax.ShapeDtypeStruct((B, E, S), F32)],
        compiler_params=_params("parallel", "parallel"),
        name="router",
    )(x1, g.reshape(1, D), mod3, wsplit)


def _cumsum_lanes(x):
    rows, S = x.shape
    tri = (lax.broadcasted_iota(jnp.int32, (LANES, LANES), 0)
           <= lax.broadcasted_iota(jnp.int32, (LANES, LANES), 1))
    tri = jnp.where(tri, 1.0, 0.0).astype(BF16)
    carry = jnp.zeros((rows, 1), F32)
    outs = []
    for c in range(S // LANES):
        inc = jnp.dot(x[:, c * LANES:(c + 1) * LANES].astype(BF16), tri, preferred_element_type=F32) + carry
        outs.append(inc)
        carry = inc[:, LANES - 1:LANES]
    return jnp.concatenate(outs, axis=1)


COMBINE_TM = 128
SLOT_CHUNK = 16
KBLOCK = 256


def _select_body(aff_ref, pos_ref, idx_ref, ts_ref, tc_ref, *, cap, tm):
    aff = aff_ref[...]
    E, S = aff.shape

    def count(mask):
        return jnp.sum(jnp.where(mask, 1.0, 0.0), axis=1, keepdims=True)

    def search(i, t):
        cand = t | jnp.left_shift(jnp.int32(1), 30 - i)
        return jnp.where(count(aff >= pltpu.bitcast(cand, F32)) >= cap, cand, t)

    thr = pltpu.bitcast(lax.fori_loop(0, 31, search, jnp.zeros((E, 1), jnp.int32)), F32)
    gt = aff > thr
    eq = aff == thr
    need = cap - count(gt)
    eq_rank = _cumsum_lanes(jnp.where(eq, 1.0, 0.0))
    sel = gt | (eq & (eq_rank <= need))
    rank = _cumsum_lanes(jnp.where(sel, 1.0, 0.0))
    sel = sel & (rank <= cap)
    sel_f = jnp.where(sel, 1.0, 0.0)
    pos = jnp.where(sel, rank - 1.0, -1.0).astype(jnp.int32)
    pos_ref[...] = pos

    tile_of = lax.broadcasted_iota(jnp.int32, (S, LANES), 0) // tm
    lane = lax.broadcasted_iota(jnp.int32, (S, LANES), 1)
    cnt = jnp.dot(sel_f.astype(BF16), jnp.where(tile_of == lane, 1.0, 0.0).astype(BF16),
                  preferred_element_type=F32)
    r = lax.broadcasted_iota(jnp.int32, (LANES, LANES), 0)
    c = lax.broadcasted_iota(jnp.int32, (LANES, LANES), 1)
    start = jnp.dot(cnt.astype(BF16), jnp.where(r < c, 1.0, 0.0).astype(BF16), preferred_element_type=F32)
    tc_ref[...] = cnt.astype(jnp.int32)
    ts_ref[...] = start.astype(jnp.int32)

    slot = lax.broadcasted_iota(jnp.int32, (cap, S), 0)
    tok = (lax.broadcasted_iota(jnp.int32, (1, S), 1) + pl.program_id(0) * S).astype(F32)
    for e in range(E):
        col = jnp.sum(jnp.where(pos[e:e + 1, :] == slot, tok, 0.0), axis=1, keepdims=True)
        idx_ref[e:e + 1, :] = jnp.broadcast_to(col, (cap, LANES)).T[0:1, :].astype(jnp.int32)


def select(aff_t, cap, tm):
    B, E, S = aff_t.shape
    assert S // tm <= LANES and tm <= 256
    blk = lambda n: pl.BlockSpec((None, E, n), lambda b: (b, 0, 0))
    return pl.pallas_call(
        functools.partial(_select_body, cap=cap, tm=tm),
        grid=(B,),
        in_specs=[blk(S)],
        out_specs=[blk(S), blk(cap), blk(LANES), blk(LANES)],
        out_shape=[jax.ShapeDtypeStruct((B, E, S), jnp.int32), jax.ShapeDtypeStruct((B, E, cap), jnp.int32),
                   jax.ShapeDtypeStruct((B, E, LANES), jnp.int32), jax.ShapeDtypeStruct((B, E, LANES), jnp.int32)],
        compiler_params=_params("parallel"),
        name="select",
    )(aff_t)


def _gather_rows_body(idx_ref, h_hbm, xe_ref, stage, sems, *, cap):
    g = pl.program_id(0)
    slot = g % 2

    def row_copy(slot, p, row):
        return pltpu.make_async_copy(h_hbm.at[pl.ds(row, 1), :], stage.at[slot, pl.ds(p, 1), :], sems.at[slot])

    def issue_group(grp, slot):
        base = grp * cap
        for p in range(cap):
            row_copy(slot, p, idx_ref[base + p]).start()

    @pl.when(g == 0)
    def _():
        issue_group(0, 0)

    @pl.when(g + 1 < pl.num_programs(0))
    def _():
        issue_group(g + 1, 1 - slot)

    def drain(p, carry):
        row_copy(slot, p, 0).wait()
        return carry

    lax.fori_loop(0, cap, drain, 0, unroll=8)
    xe_ref[...] = stage[slot].astype(xe_ref.dtype)


def gather_rows(idx, h2, cap):
    B, E, _ = idx.shape
    D = h2.shape[2]
    h2 = h2.reshape(-1, D)
    return pl.pallas_call(
        functools.partial(_gather_rows_body, cap=cap),
        grid_spec=pltpu.PrefetchScalarGridSpec(
            num_scalar_prefetch=1,
            grid=(B * E,),
            in_specs=[pl.BlockSpec(memory_space=pl.ANY)],
            out_specs=pl.BlockSpec((None, cap, D), lambda g, idx: (g % E, g // E, 0)),
            scratch_shapes=[pltpu.VMEM((2, cap, D), F32), pltpu.SemaphoreType.DMA((2,))]),
        out_shape=jax.ShapeDtypeStruct((E, B * cap, D), BF16),
        compiler_params=_params("arbitrary"),
        name="gather_rows",
    )(idx.reshape(-1), h2)


def _up_body(x_ref, w1_ref, w3_ref, o_ref):
    x = x_ref[...]
    a = jnp.dot(x, w1_ref[...].astype(BF16), preferred_element_type=F32)
    b = jnp.dot(x, w3_ref[...].astype(BF16), preferred_element_type=F32)
    o_ref[...] = (_silu(a) * b).astype(o_ref.dtype)


def expert_up(xe, w1, w3):
    E, M, D = xe.shape
    Fd = w1.shape[2]
    tm = _pick(M, 1024, 8)
    tf = _pick(Fd, 256, 2 * LANES)
    return pl.pallas_call(
        _up_body,
        grid=(E, M // tm, Fd // tf),
        in_specs=[pl.BlockSpec((None, tm, D), lambda e, m, f: (e, m, 0)),
                  pl.BlockSpec((None, D, tf), lambda e, m, f: (e, 0, f)),
                  pl.BlockSpec((None, D, tf), lambda e, m, f: (e, 0, f))],
        out_specs=pl.BlockSpec((None, tm, tf), lambda e, m, f: (e, m, f)),
        out_shape=jax.ShapeDtypeStruct((E, M, Fd), BF16),
        compiler_params=_params("parallel", "parallel", "arbitrary"),
        name="expert_up",
    )(xe, w1, w3)


def _down_body(h_ref, w2_ref, o_ref):
    o_ref[...] = jnp.dot(h_ref[...], w2_ref[...].astype(BF16), preferred_element_type=F32).astype(o_ref.dtype)


def expert_down(hid, w2):
    E, M, Fd = hid.shape
    D = w2.shape[2]
    tm = _pick(M, 2048, 8)
    tn = _pick(D, 1024, 2 * LANES)
    return pl.pallas_call(
        _down_body,
        grid=(E, M // tm, D // tn),
        in_specs=[pl.BlockSpec((None, tm, Fd), lambda e, m, n: (e, m, 0)),
                  pl.BlockSpec((None, Fd, tn), lambda e, m, n: (e, 0, n))],
        out_specs=pl.BlockSpec((None, tm, tn), lambda e, m, n: (e, m, n)),
        out_shape=jax.ShapeDtypeStruct((E, M, D), BF16),
        compiler_params=_params("parallel", "parallel", "arbitrary"),
        name="expert_down",
    )(hid, w2)


def _combine_body(ts_ref, tc_ref, pos_ref, aff_ref, ye_hbm, x_ref, mod_ref, g_ref, o_ref,
                  buf, wt, acc_sc, sems, *, cap, gate_row):
    b = pl.program_id(0)
    t = pl.program_id(1)
    n_b = pl.num_programs(0)
    n_t = pl.num_programs(1)
    E, tm = pos_ref.shape
    nt_pad = ts_ref.shape[0] // (n_b * E)
    C = SLOT_CHUNK
    shift = C.bit_length() - 1
    step = b * n_t + t
    slot = step % 2

    def chunk_copy(e, row0, slot_, k):
        return pltpu.make_async_copy(ye_hbm.at[e, pl.ds(row0, C), :], buf.at[slot_, pl.ds(k, C), :], sems.at[slot_])

    def for_each_chunk(bb, tt, fn):
        k = jnp.int32(0)
        for e in range(E):
            base = (bb * E + e) * nt_pad + tt
            first = ts_ref[base]
            n = tc_ref[base]
            first_al = lax.shift_left(lax.shift_right_logical(first, shift), shift)
            n_chunks = jnp.where(n > 0, lax.shift_right_logical(first + n - first_al + (C - 1), shift), 0)

            def chunk(c, k, e=e, first_al=first_al):
                k = pl.multiple_of(k, C)
                fn(e, first_al + c * C, k)
                return k + C

            k = lax.fori_loop(0, n_chunks, chunk, k)
        return k

    def start_tile(bb, tt, slot_):
        for_each_chunk(bb, tt, lambda e, p0, k: chunk_copy(e, pl.multiple_of(bb * cap + p0, C), slot_, k).start())

    @pl.when(step == 0)
    def _():
        buf[...] = jnp.zeros(buf.shape, buf.dtype)
        start_tile(0, 0, 0)

    @pl.when(step + 1 < n_b * n_t)
    def _():
        wrap = t + 1 == n_t
        start_tile(jnp.where(wrap, b + 1, b), jnp.where(wrap, 0, t + 1), 1 - slot)

    def weights(e, p0, k):
        rows = p0 + lax.broadcasted_iota(jnp.int32, (C, tm), 0)
        wt[pl.ds(k, C), :] = jnp.where(pos_ref[e:e + 1, :] == rows, aff_ref[e:e + 1, :], 0.0).astype(BF16)

    k = for_each_chunk(b, t, weights)
    n_used = lax.shift_right_logical(k, shift)
    n_blocks = lax.shift_right_logical(k + (KBLOCK - 1), KBLOCK.bit_length() - 1)

    def zero_tail(c, carry):
        wt[pl.ds(pl.multiple_of(c * C, C), C), :] = jnp.zeros((C, tm), BF16)
        return carry

    lax.fori_loop(n_used, n_blocks * (KBLOCK // C), zero_tail, 0)

    def drain(c, carry):
        chunk_copy(0, 0, slot, 0).wait()
        return carry

    lax.fori_loop(0, n_used, drain, 0)
    acc_sc[...] = jnp.zeros(acc_sc.shape, F32)

    def block(kb, carry):
        r0 = pl.multiple_of(kb * KBLOCK, KBLOCK)
        acc_sc[...] += lax.dot_general(wt[pl.ds(r0, KBLOCK), :], buf[slot, pl.ds(r0, KBLOCK), :], TN_DIMS,
                                       preferred_element_type=F32)
        return carry

    lax.fori_loop(0, n_blocks, block, 0)
    x2 = x_ref[...] + mod_ref[gate_row:gate_row + 1, :] * acc_sc[...]
    ms = jnp.mean(x2 * x2, axis=-1, keepdims=True)
    o_ref[...] = x2 * lax.rsqrt(ms + NORM_EPS) * g_ref[...]


def combine(tstart, tcount, pos, aff_t, ye, x1, mod3, final_g, cap, gate_row):
    B, E, S = pos.shape
    D = x1.shape[2]
    tm = min(COMBINE_TM, S)
    kcap = -(-E * (tm + SLOT_CHUNK) // KBLOCK) * KBLOCK
    return pl.pallas_call(
        functools.partial(_combine_body, cap=cap, gate_row=gate_row),
        grid_spec=pltpu.PrefetchScalarGridSpec(
            num_scalar_prefetch=2,
            grid=(B, S // tm),
            in_specs=[pl.BlockSpec((None, E, tm), lambda b, t, *_: (b, 0, t)),
                      pl.BlockSpec((None, E, tm), lambda b, t, *_: (b, 0, t)),
                      pl.BlockSpec(memory_space=pl.ANY),
                      pl.BlockSpec((None, tm, D), lambda b, t, *_: (b, t, 0)),
                      pl.BlockSpec((None, 6, D), lambda b, t, *_: (b, 0, 0)),
                      pl.BlockSpec((1, D), lambda b, t, *_: (0, 0))],
            out_specs=pl.BlockSpec((None, tm, D), lambda b, t, *_: (b, t, 0)),
            scratch_shapes=[pltpu.VMEM((2, kcap, D), BF16), pltpu.VMEM((kcap, tm), BF16),
                            pltpu.VMEM((tm, D), F32), pltpu.SemaphoreType.DMA((2,))]),
        out_shape=jax.ShapeDtypeStruct((B, S, D), F32),
        compiler_params=_params("arbitrary", "arbitrary"),
        name="combine",
    )(tstart.reshape(-1), tcount.reshape(-1), pos, aff_t, ye, x1, mod3, final_g.reshape(1, D))


def kernel(x, c, w_ada, b_ada, norm1_g, w_in, attn_q_norm_g, attn_k_norm_g, ret_decay_fwd, ret_decay_bwd,
           ret_norm_g, w_out, norm2_g, w_router, w1, w3, w2, final_g):
    B, S, D = x.shape
    T = B * S
    mix = w_out.shape[0]
    attn_w = mix // 2
    ret_w = mix - attn_w
    ah = attn_w // ATTN_HEAD_DIM
    kv_w = ATTN_KV_HEADS * ATTN_HEAD_DIM
    rh = ret_w // RET_HEAD_DIM
    E = w_router.shape[1]
    cap = CAPACITY_FACTOR * S // E

    o1 = attn_w + kv_w
    o2 = o1 + kv_w
    o3 = o2 + 2 * ret_w
    gq = _deinterleave_vec(attn_q_norm_g, ATTN_HEAD_DIM)
    gk = _deinterleave_vec(attn_k_norm_g, ATTN_HEAD_DIM)
    gain_a = jnp.concatenate([jnp.tile(gq, ah), jnp.tile(gk, ATTN_KV_HEADS)]).reshape(1, o1)
    q_scale = math.log2(math.e) / math.sqrt(ATTN_HEAD_DIM)
    cs_a = jnp.concatenate([jnp.full((attn_w,), q_scale, F32), jnp.ones((kv_w,), F32)]).reshape(1, o1)
    cs_b = jnp.concatenate([jnp.ones((ret_w,), F32), jnp.full((ret_w,), RET_HEAD_DIM ** -0.5, F32)]).reshape(1, 2 * ret_w)
    cos_a, sin_a = _rope_tables(S, ATTN_HEAD_DIM)
    cos_a2 = jnp.asarray(np.concatenate([cos_a, cos_a], axis=1))
    sin_a2 = jnp.asarray(np.concatenate([-sin_a, sin_a], axis=1))
    cos_r, sin_r = (jnp.asarray(t) for t in _rope_tables(S, RET_HEAD_DIM))

    mod3 = adaln(c, w_ada, b_ada).reshape(B, 6, D)

    h = modulate(x, norm1_g, mod3, 0, 1).reshape(T, D)
    tn = _pick(kv_w, 512, PERM_BLOCK)
    head_of = np.arange(tn) // ATTN_HEAD_DIM
    head_ones = jnp.asarray(head_of[:, None] == head_of[None, :], BF16)
    qk_a = _proj_call(_proj_attn_body, h, w_in, o1, tn, lambda j: j,
                      [_deinterleave_matrix(ATTN_HEAD_DIM), head_ones], [gain_a, cs_a], [cos_a2, sin_a2], S, "proj_attn_qk")
    qk_r = _proj_call(_proj_ret_body, h, w_in, 2 * ret_w, tn, lambda j: o2 // tn + j,
                      [_deinterleave_matrix(RET_HEAD_DIM)], [cs_b], [cos_r, sin_r], S, "proj_ret_qk")
    vg = _proj_call(_proj_plain_body, h, w_in, kv_w + 2 * ret_w, tn,
                    lambda j: jnp.where(j < kv_w // tn, o1 // tn + j, (o3 - kv_w) // tn + j), [], [], [], S, "proj_vg")
    attn_out = attention(qk_a.reshape(B, S, o1), vg.reshape(B, S, -1), ah)
    decays = jnp.stack([ret_decay_fwd, ret_decay_bwd]).astype(F32)
    v_off = kv_w // RET_HEAD_DIM
    ret_out = retention(qk_r.reshape(B, S, 2 * ret_w), vg.reshape(B, S, -1), decays, ret_norm_g.astype(F32),
                        rh, v_off, v_off + rh)
    x1 = outproj(attn_out.reshape(T, attn_w), ret_out.reshape(T, ret_w), w_out,
                 x.reshape(T, D), mod3, S, 2).reshape(B, S, D)

    h2, aff_t = router(x1, norm2_g, mod3, w_router, 3, 4)
    pos, idx, tstart, tcount = select(aff_t, cap, min(COMBINE_TM, S))
    xe = gather_rows(idx, h2, cap)
    ye = expert_down(expert_up(xe, w1, w3), w2)
    return combine(tstart, tcount, pos, aff_t, ye, x1, mod3, final_g, cap, 5)
```

```python
import functools
import math

import numpy as np
import jax
import jax.numpy as jnp
from jax import lax
from jax.experimental import pallas as pl
from jax.experimental.pallas import tpu as pltpu

F32 = jnp.float32
BF16 = jnp.bfloat16

GRID_W = 64
ROPE_THETA = 10000.0
NORM_EPS = 1e-6
ATTN_HEAD_DIM = 128
ATTN_KV_HEADS = 4
RET_HEAD_DIM = 256
RET_CHUNK = 128
CAPACITY_FACTOR = 2

LANES = 128
VMEM_LIMIT_BYTES = 56 * 1024 * 1024

NT_DIMS = (((1,), (1,)), ((), ()))
TN_DIMS = (((0,), (0,)), ((), ()))


def _params(*sem):
    return pltpu.CompilerParams(dimension_semantics=sem, vmem_limit_bytes=VMEM_LIMIT_BYTES)


def _pick(dim, pref, align=LANES):
    if dim <= pref:
        return dim
    t = (pref // align) * align
    while t >= align:
        if dim % t == 0:
            return t
        t -= align
    return dim


def _silu(v):
    return v * jax.nn.sigmoid(v)


def _adaln_body(c_ref, w_ref, b_ref, o_ref):
    c = c_ref[...]
    bp = c.shape[0]
    sc = _silu(c)
    hi = sc.astype(BF16).astype(F32)
    lhs = jnp.concatenate([hi, sc - hi], axis=0).astype(BF16)
    acc = jnp.dot(lhs, w_ref[...].astype(BF16), preferred_element_type=F32)
    o_ref[...] = acc[:bp] + acc[bp:] + b_ref[...]


def adaln(c, w_ada, b_ada):
    B, D = c.shape
    N = w_ada.shape[1]
    bp = -(-B // 8) * 8
    cp = jnp.pad(c, ((0, bp - B), (0, 0)))
    tn = _pick(N, 512)
    out = pl.pallas_call(
        _adaln_body,
        grid=(N // tn,),
        in_specs=[pl.BlockSpec((bp, D), lambda j: (0, 0)),
                  pl.BlockSpec((D, tn), lambda j: (0, j)),
                  pl.BlockSpec((1, tn), lambda j: (0, j))],
        out_specs=pl.BlockSpec((bp, tn), lambda j: (0, j)),
        out_shape=jax.ShapeDtypeStruct((bp, N), F32),
        compiler_params=_params("arbitrary"),
        name="adaln",
    )(cp, w_ada, b_ada.reshape(1, N))
    return out[:B]


def _modulated_norm(x, g, shift, scale):
    ms = jnp.mean(x * x, axis=-1, keepdims=True)
    return x * lax.rsqrt(ms + NORM_EPS) * g * (1.0 + scale) + shift


def _modulate_body(x_ref, g_ref, mod_ref, o_ref, *, shift_row, scale_row):
    h = _modulated_norm(x_ref[...], g_ref[...], mod_ref[shift_row:shift_row + 1, :],
                        mod_ref[scale_row:scale_row + 1, :])
    o_ref[...] = h.astype(o_ref.dtype)


def modulate(x, g, mod3, shift_row, scale_row):
    B, S, D = x.shape
    ts = _pick(S, 512, 8)
    return pl.pallas_call(
        functools.partial(_modulate_body, shift_row=shift_row, scale_row=scale_row),
        grid=(B, S // ts),
        in_specs=[pl.BlockSpec((None, ts, D), lambda b, i: (b, i, 0)),
                  pl.BlockSpec((1, D), lambda b, i: (0, 0)),
                  pl.BlockSpec((None, 6, D), lambda b, i: (b, 0, 0))],
        out_specs=pl.BlockSpec((None, ts, D), lambda b, i: (b, i, 0)),
        out_shape=jax.ShapeDtypeStruct((B, S, D), BF16),
        compiler_params=_params("parallel", "parallel"),
        name="modulate",
    )(x, g.reshape(1, D), mod3)


PERM_BLOCK = 256


def _stage_weights(w_refs, perm_ref, wb_sc):
    @pl.when(pl.program_id(1) == 0)
    def _():
        r0 = 0
        for w_ref in w_refs:
            rows = slice(r0, r0 + w_ref.shape[0])
            r0 += w_ref.shape[0]
            if perm_ref is None:
                wb_sc[rows, :] = w_ref[...].astype(BF16)
            else:
                for c in range(w_ref.shape[1] // PERM_BLOCK):
                    cols = slice(c * PERM_BLOCK, (c + 1) * PERM_BLOCK)
                    wb_sc[rows, cols] = jnp.dot(w_ref[:, cols].astype(BF16), perm_ref[...],
                                                preferred_element_type=F32).astype(BF16)


def _proj_attn_body(h_ref, w_ref, perm_ref, ones_ref, g_ref, cs_ref, cos_ref, sin_ref, o_ref, wb_sc):
    _stage_weights([w_ref], perm_ref, wb_sc)
    acc = jnp.dot(h_ref[...], wb_sc[...], preferred_element_type=F32)
    d = ATTN_HEAD_DIM
    ssq = jnp.dot((acc * acc).astype(BF16), ones_ref[...], preferred_element_type=F32)
    y = acc * lax.rsqrt(ssq * (1.0 / d) + NORM_EPS) * g_ref[...]
    cosv = cos_ref[...]
    sinv = sin_ref[...]
    for hh in range(acc.shape[1] // d):
        sl = slice(hh * d, (hh + 1) * d)
        yh = y[:, sl]
        rot = pltpu.roll(yh, d // 2, axis=1)
        o_ref[:, sl] = ((yh * cosv + rot * sinv) * cs_ref[:, sl]).astype(o_ref.dtype)


def _proj_ret_body(h_ref, w_ref, perm_ref, cs_ref, cos_ref, sin_ref, o_ref, wb_sc):
    _stage_weights([w_ref], perm_ref, wb_sc)
    acc = jnp.dot(h_ref[...], wb_sc[...], preferred_element_type=F32)
    cosv = cos_ref[...]
    sinv = sin_ref[...]
    d = RET_HEAD_DIM
    for hh in range(acc.shape[1] // d):
        s1 = slice(hh * d, hh * d + d // 2)
        s2 = slice(hh * d + d // 2, (hh + 1) * d)
        x1 = acc[:, s1]
        x2 = acc[:, s2]
        o_ref[:, s1] = ((x1 * cosv - x2 * sinv) * cs_ref[:, s1]).astype(o_ref.dtype)
        o_ref[:, s2] = ((x1 * sinv + x2 * cosv) * cs_ref[:, s2]).astype(o_ref.dtype)


def _proj_plain_body(h_ref, w_ref, o_ref, wb_sc):
    _stage_weights([w_ref], None, wb_sc)
    o_ref[...] = jnp.dot(h_ref[...], wb_sc[...], preferred_element_type=F32).astype(o_ref.dtype)


def _proj_call(body, h2d, w, n_out, tn, in_block, mats, rows, tables, S, name):
    T, K = h2d.shape
    tm = _pick(S, 1024, 8)
    per_b = S // tm
    in_specs = [pl.BlockSpec((tm, K), lambda j, i: (i, 0)),
                pl.BlockSpec((K, tn), lambda j, i: (0, in_block(j)))]
    in_specs += [pl.BlockSpec(m.shape, lambda j, i: (0, 0)) for m in mats]
    in_specs += [pl.BlockSpec((1, tn), lambda j, i: (0, j)) for _ in rows]
    in_specs += [pl.BlockSpec((tm, t.shape[1]), lambda j, i: (i % per_b, 0)) for t in tables]
    return pl.pallas_call(
        body,
        grid=(n_out // tn, T // tm),
        in_specs=in_specs,
        out_specs=pl.BlockSpec((tm, tn), lambda j, i: (i, j)),
        out_shape=jax.ShapeDtypeStruct((T, n_out), BF16),
        scratch_shapes=[pltpu.VMEM((K, tn), BF16)],
        compiler_params=_params("arbitrary", "arbitrary"),
        name=name,
    )(h2d, w, *mats, *rows, *tables)


def _rope_tables(S, d):
    quarter = d // 4
    t = np.arange(S)
    inv = ROPE_THETA ** (-np.arange(quarter, dtype=np.float64) / quarter)
    ang = np.concatenate([(t // GRID_W)[:, None] * inv, (t % GRID_W)[:, None] * inv], axis=-1)
    return np.cos(ang).astype(np.float32), np.sin(ang).astype(np.float32)


def _deinterleave_vec(g, d):
    return g.reshape(-1, d // 2, 2).transpose(0, 2, 1).reshape(-1)


def _deinterleave_matrix(d):
    src = np.concatenate([h * d + np.concatenate([np.arange(0, d, 2), np.arange(1, d, 2)])
                          for h in range(PERM_BLOCK // d)])
    p = np.zeros((PERM_BLOCK, PERM_BLOCK), np.float32)
    p[src, np.arange(PERM_BLOCK)] = 1.0
    return jnp.asarray(p, BF16)


ATTN_TQ = 256
ATTN_SUB = 2
ATTN_TK = 512
ONES_ROWS = 16
ATTN_SAFE_EXP = 50.0


def _attn_body(q_ref, k_ref, v_ref, o_ref, vt_sc, qt_sc, kmax_sc, m_sc, acc_sc, *, tq, tk, groups):
    S, dh = k_ref.shape
    nk = S // tk

    @pl.when(pl.program_id(2) == 0)
    def _():
        for j in range(nk):
            vt_sc[j, :dh, :] = v_ref[j * tk:(j + 1) * tk, :].astype(F32).T.astype(BF16)
            vt_sc[j, dh:, :] = jnp.ones((ONES_ROWS, tk), BF16)
        kf = k_ref[...].astype(F32)
        ksq = jnp.max(jnp.sum(kf * kf, axis=1, keepdims=True), axis=0, keepdims=True)
        kmax_sc[...] = jnp.broadcast_to(jnp.sqrt(ksq), kmax_sc.shape)

    for r0 in range(0, q_ref.shape[0], tq):
        _attn_queries(q_ref.at[r0:r0 + tq, :], k_ref, o_ref.at[r0:r0 + tq, :], vt_sc, qt_sc, kmax_sc, m_sc, acc_sc,
                      tk=tk, groups=groups)


def _attn_queries(q_ref, k_ref, o_ref, vt_sc, qt_sc, kmax_sc, m_sc, acc_sc, *, tk, groups):
    tq = q_ref.shape[0]
    S, dh = k_ref.shape
    nk = S // tk
    for g in range(groups):
        qt = q_ref[:, g * dh:(g + 1) * dh].astype(F32).T
        qt_sc[:, g * tq:(g + 1) * tq] = qt.astype(BF16)
        m_sc[:, g * tq:(g + 1) * tq] = jnp.sqrt(jnp.sum(qt * qt, axis=0, keepdims=True)) * kmax_sc[:, 0:1]
    bounded = jnp.max(m_sc[...]) <= ATTN_SAFE_EXP

    @pl.when(bounded)
    def _():
        for j in range(nk):
            st = jnp.dot(k_ref[j * tk:(j + 1) * tk, :], qt_sc[...], preferred_element_type=F32)
            p = jnp.exp2(st - m_sc[...]).astype(BF16)
            pv = jnp.dot(vt_sc[j], p, preferred_element_type=F32)
            acc_sc[...] = pv if j == 0 else acc_sc[...] + pv

    @pl.when(jnp.logical_not(bounded))
    def _():
        m_sc[...] = jnp.full(m_sc.shape, -jnp.inf, F32)
        acc_sc[...] = jnp.zeros(acc_sc.shape, F32)

        def step(j, carry):
            off = pl.multiple_of(j * tk, tk)
            st = jnp.dot(k_ref[pl.ds(off, tk), :], qt_sc[...], preferred_element_type=F32)
            m_prev = m_sc[...]
            m_new = jnp.maximum(m_prev, jnp.max(st, axis=0, keepdims=True))
            p = jnp.exp2(st - m_new).astype(BF16)
            acc_sc[...] = acc_sc[...] * jnp.exp2(m_prev - m_new) + jnp.dot(vt_sc[j], p, preferred_element_type=F32)
            m_sc[...] = m_new
            return carry

        lax.fori_loop(0, nk, step, 0)

    ot = acc_sc[:dh, :] / acc_sc[dh:dh + 1, :]
    for g in range(groups):
        o_ref[:, g * dh:(g + 1) * dh] = ot[:, g * tq:(g + 1) * tq].T.astype(o_ref.dtype)


def attention(qk, vg, n_heads):
    B, S, _ = qk.shape
    dh = ATTN_HEAD_DIM
    kvh = ATTN_KV_HEADS
    groups = n_heads // kvh
    tq = _pick(S, ATTN_TQ, 8)
    tk = _pick(S, ATTN_TK, 8)
    tb = _pick(S, ATTN_SUB * tq, tq)
    gw = groups * dh
    return pl.pallas_call(
        functools.partial(_attn_body, tq=tq, tk=tk, groups=groups),
        grid=(B, kvh, S // tb),
        in_specs=[pl.BlockSpec((None, tb, gw), lambda b, k, i: (b, i, k)),
                  pl.BlockSpec((None, S, dh), lambda b, k, i: (b, 0, n_heads + k)),
                  pl.BlockSpec((None, S, dh), lambda b, k, i: (b, 0, k))],
        out_specs=pl.BlockSpec((None, tb, gw), lambda b, k, i: (b, i, k)),
        out_shape=jax.ShapeDtypeStruct((B, S, n_heads * dh), BF16),
        scratch_shapes=[pltpu.VMEM((S // tk, dh + ONES_ROWS, tk), BF16),
                        pltpu.VMEM((dh, groups * tq), BF16),
                        pltpu.VMEM((1, LANES), F32),
                        pltpu.VMEM((1, groups * tq), F32),
                        pltpu.VMEM((dh + ONES_ROWS, groups * tq), F32)],
        compiler_params=_params("arbitrary", "arbitrary", "arbitrary"),
        name="attention",
    )(qk, qk, vg)


RET_UNROLL = 8


def _ret_body(dec_ref, q_ref, k_ref, v_ref, gr_ref, g_ref, o_ref, rf_sc, rb_sc, racc_sc, *, chunk):
    C = chunk
    S, dk = q_ref.shape
    n = S // C
    h = pl.program_id(1)
    lgf = -jnp.exp(jnp.full((1, 1), dec_ref[0, h], F32))
    lgb = -jnp.exp(jnp.full((1, 1), dec_ref[1, h], F32))
    idx = lax.broadcasted_iota(jnp.int32, (C, 1), 0).astype(F32)
    zeta_f = jnp.exp(lgf * (C - 1.0 - idx))
    xi_f = jnp.exp(lgf * (idx + 1.0))
    zeta_b = jnp.exp(lgb * idx)
    xi_b = jnp.exp(lgb * (C - idx))
    cd_f = jnp.exp(lgf * C)
    cd_b = jnp.exp(lgb * C)
    diff = (lax.broadcasted_iota(jnp.int32, (C, C), 0) - lax.broadcasted_iota(jnp.int32, (C, C), 1)).astype(F32)
    decay = jnp.where(diff >= 0, jnp.exp(lgf * jnp.maximum(diff, 0.0)), jnp.exp(lgb * jnp.maximum(-diff, 0.0)))

    def kv_update(i, zeta, cd):
        off = pl.multiple_of(i * C, C)
        kz = (k_ref[pl.ds(off, C), :].astype(F32) * zeta).astype(BF16)
        kv = lax.dot_general(kz, v_ref[pl.ds(off, C), :], TN_DIMS, preferred_element_type=F32)
        racc_sc[...] = racc_sc[...] * cd + kv

    racc_sc[...] = jnp.zeros(racc_sc.shape, F32)

    def fstep(i, carry):
        rf_sc[i] = racc_sc[...].astype(BF16)
        kv_update(i, zeta_f, cd_f)
        return carry

    lax.fori_loop(0, n, fstep, 0, unroll=RET_UNROLL)
    racc_sc[...] = jnp.zeros(racc_sc.shape, F32)

    def bstep(t, carry):
        i = n - 1 - t
        rb_sc[i] = racc_sc[...].astype(BF16)
        kv_update(i, zeta_b, cd_b)
        return carry

    lax.fori_loop(0, n, bstep, 0, unroll=RET_UNROLL)
    gain = g_ref[...]

    def ostep(i, carry):
        off = pl.multiple_of(i * C, C)
        qi = q_ref[pl.ds(off, C), :]
        ki = k_ref[pl.ds(off, C), :]
        vi = v_ref[pl.ds(off, C), :]
        s = lax.dot_general(qi, ki, NT_DIMS, preferred_element_type=F32) * decay
        qf = (qi.astype(F32) * xi_f).astype(BF16)
        qb = (qi.astype(F32) * xi_b).astype(BF16)
        o = (jnp.dot(s.astype(BF16), vi, preferred_element_type=F32)
             + jnp.dot(qf, rf_sc[i], preferred_element_type=F32)
             + jnp.dot(qb, rb_sc[i], preferred_element_type=F32))
        mu = jnp.mean(o, axis=-1, keepdims=True)
        d = o - mu
        var = jnp.mean(d * d, axis=-1, keepdims=True)
        y = d * lax.rsqrt(var + NORM_EPS) * gain
        o_ref[pl.ds(off, C), :] = (y * _silu(gr_ref[pl.ds(off, C), :].astype(F32))).astype(o_ref.dtype)
        return carry

    lax.fori_loop(0, n, ostep, 0, unroll=RET_UNROLL)


def retention(qk, vg, decays, gain, n_heads, v_off, g_off):
    B, S, _ = qk.shape
    d = RET_HEAD_DIM
    n = S // RET_CHUNK
    blk = lambda off: pl.BlockSpec((None, S, d), lambda b, h: (b, 0, off + h))
    return pl.pallas_call(
        functools.partial(_ret_body, chunk=RET_CHUNK),
        grid=(B, n_heads),
        in_specs=[pl.BlockSpec(memory_space=pltpu.SMEM),
                  blk(0), blk(n_heads), blk(v_off), blk(g_off),
                  pl.BlockSpec((None, 1, d), lambda b, h: (h, 0, 0))],
        out_specs=blk(0),
        out_shape=jax.ShapeDtypeStruct((B, S, n_heads * d), BF16),
        scratch_shapes=[pltpu.VMEM((n, d, d), BF16), pltpu.VMEM((n, d, d), BF16), pltpu.VMEM((d, d), F32)],
        compiler_params=_params("parallel", "parallel"),
        name="retention",
    )(decays, qk, qk, vg, vg, gain.reshape(n_heads, 1, d))


def _outproj_body(a_ref, r_ref, wa_ref, wr_ref, x_ref, mod_ref, o_ref, wb_sc, *, gate_row):
    _stage_weights([wa_ref, wr_ref], None, wb_sc)
    ka = a_ref.shape[1]
    acc = (jnp.dot(a_ref[...], wb_sc[:ka, :], preferred_element_type=F32)
           + jnp.dot(r_ref[...], wb_sc[ka:, :], preferred_element_type=F32))
    o_ref[...] = x_ref[...] + mod_ref[gate_row:gate_row + 1, :] * acc


def outproj(attn_out, ret_out, w_o, x2d, mod3, S, gate_row):
    T, Ka = attn_out.shape
    Kr = ret_out.shape[1]
    assert Ka == Kr and w_o.shape[0] == Ka + Kr
    N = w_o.shape[1]
    tm = _pick(S, 1024, 8)
    tn = _pick(N, 512, 2 * LANES)
    per_b = S // tm
    return pl.pallas_call(
        functools.partial(_outproj_body, gate_row=gate_row),
        grid=(N // tn, T // tm),
        in_specs=[pl.BlockSpec((tm, Ka), lambda j, i: (i, 0)),
                  pl.BlockSpec((tm, Kr), lambda j, i: (i, 0)),
                  pl.BlockSpec((Ka, tn), lambda j, i: (0, j)),
                  pl.BlockSpec((Kr, tn), lambda j, i: (1, j)),
                  pl.BlockSpec((tm, tn), lambda j, i: (i, j)),
                  pl.BlockSpec((None, 6, tn), lambda j, i: (i // per_b, 0, j))],
        out_specs=pl.BlockSpec((tm, tn), lambda j, i: (i, j)),
        out_shape=jax.ShapeDtypeStruct((T, N), F32),
        scratch_shapes=[pltpu.VMEM((Ka + Kr, tn), BF16)],
        compiler_params=_params("arbitrary", "arbitrary"),
        name="outproj",
    )(attn_out, ret_out, w_o, w_o, x2d, mod3)


def _router_body(x_ref, g_ref, mod_ref, wr_ref, h_ref, aff_ref, *, shift_row, scale_row):
    h = _modulated_norm(x_ref[...], g_ref[...], mod_ref[shift_row:shift_row + 1, :],
                        mod_ref[scale_row:scale_row + 1, :])
    h_ref[...] = h
    hi = h.astype(BF16)
    lo = (h - hi.astype(F32)).astype(BF16)
    whi = wr_ref[0]
    wlo = wr_ref[1]
    lg = (lax.dot_general(whi, hi, NT_DIMS, preferred_element_type=F32)
          + lax.dot_general(whi, lo, NT_DIMS, preferred_element_type=F32)
          + lax.dot_general(wlo, hi, NT_DIMS, preferred_element_type=F32))
    e = jnp.exp(lg - jnp.max(lg, axis=0, keepdims=True))
    aff_ref[...] = e / jnp.sum(e, axis=0, keepdims=True)


def router(x1, g, mod3, w_router, shift_row, scale_row):
    B, S, D = x1.shape
    E = w_router.shape[1]
    wt = w_router.T
    whi = wt.astype(BF16)
    wsplit = jnp.stack([whi, (wt - whi.astype(F32)).astype(BF16)])
    ts = _pick(S, 512)
    return pl.pallas_call(
        functools.partial(_router_body, shift_row=shift_row, scale_row=scale_row),
        grid=(B, S // ts),
        in_specs=[pl.BlockSpec((None, ts, D), lambda b, i: (b, i, 0)),
                  pl.BlockSpec((1, D), lambda b, i: (0, 0)),
                  pl.BlockSpec((None, 6, D), lambda b, i: (b, 0, 0)),
                  pl.BlockSpec((2, E, D), lambda b, i: (0, 0, 0))],
        out_specs=[pl.BlockSpec((None, ts, D), lambda b, i: (b, i, 0)),
                   pl.BlockSpec((None, E, ts), lambda b, i: (b, 0, i))],
        out_shape=[jax.ShapeDtypeStruct((B, S, D), F32), jax.ShapeDtypeStruct((B, E, S), F32)],
        compiler_params=_params("parallel", "parallel"),
        name="router",
    )(x1, g.reshape(1, D), mod3, wsplit)


def _cumsum_lanes(x):
    rows, S = x.shape
    tri = (lax.broadcasted_iota(jnp.int32, (LANES, LANES), 0)
           <= lax.broadcasted_iota(jnp.int32, (LANES, LANES), 1))
    tri = jnp.where(tri, 1.0, 0.0).astype(BF16)
    carry = jnp.zeros((rows, 1), F32)
    outs = []
    for c in range(S // LANES):
        inc = jnp.dot(x[:, c * LANES:(c + 1) * LANES].astype(BF16), tri, preferred_element_type=F32) + carry
        outs.append(inc)
        carry = inc[:, LANES - 1:LANES]
    return jnp.concatenate(outs, axis=1)


COMBINE_TM = 128
SLOT_CHUNK = 16
KBLOCK = 256


def _select_body(aff_ref, pos_ref, idx_ref, ts_ref, tc_ref, *, cap, tm):
    aff = aff_ref[...]
    E, S = aff.shape

    def count(mask):
        return jnp.sum(jnp.where(mask, 1.0, 0.0), axis=1, keepdims=True)

    def search(i, t):
        cand = t | jnp.left_shift(jnp.int32(1), 30 - i)
        return jnp.where(count(aff >= pltpu.bitcast(cand, F32)) >= cap, cand, t)

    thr = pltpu.bitcast(lax.fori_loop(0, 31, search, jnp.zeros((E, 1), jnp.int32)), F32)
    gt = aff > thr
    eq = aff == thr
    need = cap - count(gt)
    eq_rank = _cumsum_lanes(jnp.where(eq, 1.0, 0.0))
    sel = gt | (eq & (eq_rank <= need))
    rank = _cumsum_lanes(jnp.where(sel, 1.0, 0.0))
    sel = sel & (rank <= cap)
    sel_f = jnp.where(sel, 1.0, 0.0)
    pos = jnp.where(sel, rank - 1.0, -1.0).astype(jnp.int32)
    pos_ref[...] = pos

    tile_of = lax.broadcasted_iota(jnp.int32, (S, LANES), 0) // tm
    lane = lax.broadcasted_iota(jnp.int32, (S, LANES), 1)
    cnt = jnp.dot(sel_f.astype(BF16), jnp.where(tile_of == lane, 1.0, 0.0).astype(BF16),
                  preferred_element_type=F32)
    r = lax.broadcasted_iota(jnp.int32, (LANES, LANES), 0)
    c = lax.broadcasted_iota(jnp.int32, (LANES, LANES), 1)
    start = jnp.dot(cnt.astype(BF16), jnp.where(r < c, 1.0, 0.0).astype(BF16), preferred_element_type=F32)
    tc_ref[...] = cnt.astype(jnp.int32)
    ts_ref[...] = start.astype(jnp.int32)

    slot = lax.broadcasted_iota(jnp.int32, (cap, S), 0)
    tok = (lax.broadcasted_iota(jnp.int32, (1, S), 1) + pl.program_id(0) * S).astype(F32)
    for e in range(E):
        col = jnp.sum(jnp.where(pos[e:e + 1, :] == slot, tok, 0.0), axis=1, keepdims=True)
        idx_ref[e:e + 1, :] = jnp.broadcast_to(col, (cap, LANES)).T[0:1, :].astype(jnp.int32)


def select(aff_t, cap, tm):
    B, E, S = aff_t.shape
    assert S // tm <= LANES and tm <= 256
    blk = lambda n: pl.BlockSpec((None, E, n), lambda b: (b, 0, 0))
    return pl.pallas_call(
        functools.partial(_select_body, cap=cap, tm=tm),
        grid=(B,),
        in_specs=[blk(S)],
        out_specs=[blk(S), blk(cap), blk(LANES), blk(LANES)],
        out_shape=[jax.ShapeDtypeStruct((B, E, S), jnp.int32), jax.ShapeDtypeStruct((B, E, cap), jnp.int32),
                   jax.ShapeDtypeStruct((B, E, LANES), jnp.int32), jax.ShapeDtypeStruct((B, E, LANES), jnp.int32)],
        compiler_params=_params("parallel"),
        name="select",
    )(aff_t)


def _gather_rows_body(idx_ref, h_hbm, xe_ref, stage, sems, *, cap):
    g = pl.program_id(0)
    slot = g % 2

    def row_copy(slot, p, row):
        return pltpu.make_async_copy(h_hbm.at[pl.ds(row, 1), :], stage.at[slot, pl.ds(p, 1), :], sems.at[slot])

    def issue_group(grp, slot):
        base = grp * cap
        for p in range(cap):
            row_copy(slot, p, idx_ref[base + p]).start()

    @pl.when(g == 0)
    def _():
        issue_group(0, 0)

    @pl.when(g + 1 < pl.num_programs(0))
    def _():
        issue_group(g + 1, 1 - slot)

    def drain(p, carry):
        row_copy(slot, p, 0).wait()
        return carry

    lax.fori_loop(0, cap, drain, 0, unroll=8)
    xe_ref[...] = stage[slot].astype(xe_ref.dtype)


def gather_rows(idx, h2, cap):
    B, E, _ = idx.shape
    D = h2.shape[2]
    h2 = h2.reshape(-1, D)
    return pl.pallas_call(
        functools.partial(_gather_rows_body, cap=cap),
        grid_spec=pltpu.PrefetchScalarGridSpec(
            num_scalar_prefetch=1,
            grid=(B * E,),
            in_specs=[pl.BlockSpec(memory_space=pl.ANY)],
            out_specs=pl.BlockSpec((None, cap, D), lambda g, idx: (g % E, g // E, 0)),
            scratch_shapes=[pltpu.VMEM((2, cap, D), F32), pltpu.SemaphoreType.DMA((2,))]),
        out_shape=jax.ShapeDtypeStruct((E, B * cap, D), BF16),
        compiler_params=_params("arbitrary"),
        name="gather_rows",
    )(idx.reshape(-1), h2)


def _up_body(x_ref, w1_ref, w3_ref, o_ref):
    x = x_ref[...]
    a = jnp.dot(x, w1_ref[...].astype(BF16), preferred_element_type=F32)
    b = jnp.dot(x, w3_ref[...].astype(BF16), preferred_element_type=F32)
    o_ref[...] = (_silu(a) * b).astype(o_ref.dtype)


def expert_up(xe, w1, w3):
    E, M, D = xe.shape
    Fd = w1.shape[2]
    tm = _pick(M, 2048, 8)
    tf = _pick(Fd, 256, 2 * LANES)
    return pl.pallas_call(
        _up_body,
        grid=(E, M // tm, Fd // tf),
        in_specs=[pl.BlockSpec((None, tm, D), lambda e, m, f: (e, m, 0)),
                  pl.BlockSpec((None, D, tf), lambda e, m, f: (e, 0, f)),
                  pl.BlockSpec((None, D, tf), lambda e, m, f: (e, 0, f))],
        out_specs=pl.BlockSpec((None, tm, tf), lambda e, m, f: (e, m, f)),
        out_shape=jax.ShapeDtypeStruct((E, M, Fd), BF16),
        compiler_params=_params("parallel", "parallel", "arbitrary"),
        name="expert_up",
    )(xe, w1, w3)


def _down_body(h_ref, w2_ref, o_ref):
    o_ref[...] = jnp.dot(h_ref[...], w2_ref[...].astype(BF16), preferred_element_type=F32).astype(o_ref.dtype)


def expert_down(hid, w2):
    E, M, Fd = hid.shape
    D = w2.shape[2]
    tm = _pick(M, 2048, 8)
    tn = _pick(D, 1024, 2 * LANES)
    return pl.pallas_call(
        _down_body,
        grid=(E, M // tm, D // tn),
        in_specs=[pl.BlockSpec((None, tm, Fd), lambda e, m, n: (e, m, 0)),
                  pl.BlockSpec((None, Fd, tn), lambda e, m, n: (e, 0, n))],
        out_specs=pl.BlockSpec((None, tm, tn), lambda e, m, n: (e, m, n)),
        out_shape=jax.ShapeDtypeStruct((E, M, D), BF16),
        compiler_params=_params("parallel", "parallel", "arbitrary"),
        name="expert_down",
    )(hid, w2)


def _combine_body(ts_ref, tc_ref, pos_ref, aff_ref, ye_hbm, x_ref, mod_ref, g_ref, o_ref,
                  buf, wt, acc_sc, sems, *, cap, gate_row):
    b = pl.program_id(0)
    t = pl.program_id(1)
    n_b = pl.num_programs(0)
    n_t = pl.num_programs(1)
    E, tm = pos_ref.shape
    nt_pad = ts_ref.shape[0] // (n_b * E)
    C = SLOT_CHUNK
    shift = C.bit_length() - 1
    step = b * n_t + t
    slot = step % 2

    def chunk_copy(e, row0, slot_, k):
        return pltpu.make_async_copy(ye_hbm.at[e, pl.ds(row0, C), :], buf.at[slot_, pl.ds(k, C), :], sems.at[slot_])

    def for_each_chunk(bb, tt, fn):
        k = jnp.int32(0)
        for e in range(E):
            base = (bb * E + e) * nt_pad + tt
            first = ts_ref[base]
            n = tc_ref[base]
            first_al = lax.shift_left(lax.shift_right_logical(first, shift), shift)
            n_chunks = jnp.where(n > 0, lax.shift_right_logical(first + n - first_al + (C - 1), shift), 0)

            def chunk(c, k, e=e, first_al=first_al):
                k = pl.multiple_of(k, C)
                fn(e, first_al + c * C, k)
                return k + C

            k = lax.fori_loop(0, n_chunks, chunk, k)
        return k

    def start_tile(bb, tt, slot_):
        for_each_chunk(bb, tt, lambda e, p0, k: chunk_copy(e, pl.multiple_of(bb * cap + p0, C), slot_, k).start())

    @pl.when(step == 0)
    def _():
        buf[...] = jnp.zeros(buf.shape, buf.dtype)
        start_tile(0, 0, 0)

    @pl.when(step + 1 < n_b * n_t)
    def _():
        wrap = t + 1 == n_t
        start_tile(jnp.where(wrap, b + 1, b), jnp.where(wrap, 0, t + 1), 1 - slot)

    def weights(e, p0, k):
        rows = p0 + lax.broadcasted_iota(jnp.int32, (C, tm), 0)
        wt[pl.ds(k, C), :] = jnp.where(pos_ref[e:e + 1, :] == rows, aff_ref[e:e + 1, :], 0.0).astype(BF16)

    k = for_each_chunk(b, t, weights)
    n_used = lax.shift_right_logical(k, shift)
    n_blocks = lax.shift_right_logical(k + (KBLOCK - 1), KBLOCK.bit_length() - 1)

    def zero_tail(c, carry):
        wt[pl.ds(pl.multiple_of(c * C, C), C), :] = jnp.zeros((C, tm), BF16)
        return carry

    lax.fori_loop(n_used, n_blocks * (KBLOCK // C), zero_tail, 0)

    def drain(c, carry):
        chunk_copy(0, 0, slot, 0).wait()
        return carry

    lax.fori_loop(0, n_used, drain, 0)
    acc_sc[...] = jnp.zeros(acc_sc.shape, F32)

    def block(kb, carry):
        r0 = pl.multiple_of(kb * KBLOCK, KBLOCK)
        acc_sc[...] += lax.dot_general(wt[pl.ds(r0, KBLOCK), :], buf[slot, pl.ds(r0, KBLOCK), :], TN_DIMS,
                                       preferred_element_type=F32)
        return carry

    lax.fori_loop(0, n_blocks, block, 0)
    x2 = x_ref[...] + mod_ref[gate_row:gate_row + 1, :] * acc_sc[...]
    ms = jnp.mean(x2 * x2, axis=-1, keepdims=True)
    o_ref[...] = x2 * lax.rsqrt(ms + NORM_EPS) * g_ref[...]


def combine(tstart, tcount, pos, aff_t, ye, x1, mod3, final_g, cap, gate_row):
    B, E, S = pos.shape
    D = x1.shape[2]
    tm = min(COMBINE_TM, S)
    kcap = -(-E * (tm + SLOT_CHUNK) // KBLOCK) * KBLOCK
    return pl.pallas_call(
        functools.partial(_combine_body, cap=cap, gate_row=gate_row),
        grid_spec=pltpu.PrefetchScalarGridSpec(
            num_scalar_prefetch=2,
            grid=(B, S // tm),
            in_specs=[pl.BlockSpec((None, E, tm), lambda b, t, *_: (b, 0, t)),
                      pl.BlockSpec((None, E, tm), lambda b, t, *_: (b, 0, t)),
                      pl.BlockSpec(memory_space=pl.ANY),
                      pl.BlockSpec((None, tm, D), lambda b, t, *_: (b, t, 0)),
                      pl.BlockSpec((None, 6, D), lambda b, t, *_: (b, 0, 0)),
                      pl.BlockSpec((1, D), lambda b, t, *_: (0, 0))],
            out_specs=pl.BlockSpec((None, tm, D), lambda b, t, *_: (b, t, 0)),
            scratch_shapes=[pltpu.VMEM((2, kcap, D), BF16), pltpu.VMEM((kcap, tm), BF16),
                            pltpu.VMEM((tm, D), F32), pltpu.SemaphoreType.DMA((2,))]),
        out_shape=jax.ShapeDtypeStruct((B, S, D), F32),
        compiler_params=_params("arbitrary", "arbitrary"),
        name="combine",
    )(tstart.reshape(-1), tcount.reshape(-1), pos, aff_t, ye, x1, mod3, final_g.reshape(1, D))


def kernel(x, c, w_ada, b_ada, norm1_g, w_in, attn_q_norm_g, attn_k_norm_g, ret_decay_fwd, ret_decay_bwd,
           ret_norm_g, w_out, norm2_g, w_router, w1, w3, w2, final_g):
    B, S, D = x.shape
    T = B * S
    mix = w_out.shape[0]
    attn_w = mix // 2
    ret_w = mix - attn_w
    ah = attn_w // ATTN_HEAD_DIM
    kv_w = ATTN_KV_HEADS * ATTN_HEAD_DIM
    rh = ret_w // RET_HEAD_DIM
    E = w_router.shape[1]
    cap = CAPACITY_FACTOR * S // E

    o1 = attn_w + kv_w
    o2 = o1 + kv_w
    o3 = o2 + 2 * ret_w
    gq = _deinterleave_vec(attn_q_norm_g, ATTN_HEAD_DIM)
    gk = _deinterleave_vec(attn_k_norm_g, ATTN_HEAD_DIM)
    gain_a = jnp.concatenate([jnp.tile(gq, ah), jnp.tile(gk, ATTN_KV_HEADS)]).reshape(1, o1)
    q_scale = math.log2(math.e) / math.sqrt(ATTN_HEAD_DIM)
    cs_a = jnp.concatenate([jnp.full((attn_w,), q_scale, F32), jnp.ones((kv_w,), F32)]).reshape(1, o1)
    cs_b = jnp.concatenate([jnp.ones((ret_w,), F32), jnp.full((ret_w,), RET_HEAD_DIM ** -0.5, F32)]).reshape(1, 2 * ret_w)
    cos_a, sin_a = _rope_tables(S, ATTN_HEAD_DIM)
    cos_a2 = jnp.asarray(np.concatenate([cos_a, cos_a], axis=1))
    sin_a2 = jnp.asarray(np.concatenate([-sin_a, sin_a], axis=1))
    cos_r, sin_r = (jnp.asarray(t) for t in _rope_tables(S, RET_HEAD_DIM))

    mod3 = adaln(c, w_ada, b_ada).reshape(B, 6, D)

    h = modulate(x, norm1_g, mod3, 0, 1).reshape(T, D)
    tn = _pick(kv_w, 512, PERM_BLOCK)
    head_of = np.arange(tn) // ATTN_HEAD_DIM
    head_ones = jnp.asarray(head_of[:, None] == head_of[None, :], BF16)
    qk_a = _proj_call(_proj_attn_body, h, w_in, o1, tn, lambda j: j,
                      [_deinterleave_matrix(ATTN_HEAD_DIM), head_ones], [gain_a, cs_a], [cos_a2, sin_a2], S, "proj_attn_qk")
    qk_r = _proj_call(_proj_ret_body, h, w_in, 2 * ret_w, tn, lambda j: o2 // tn + j,
                      [_deinterleave_matrix(RET_HEAD_DIM)], [cs_b], [cos_r, sin_r], S, "proj_ret_qk")
    vg = _proj_call(_proj_plain_body, h, w_in, kv_w + 2 * ret_w, tn,
                    lambda j: jnp.where(j < kv_w // tn, o1 // tn + j, (o3 - kv_w) // tn + j), [], [], [], S, "proj_vg")
    attn_out = attention(qk_a.reshape(B, S, o1), vg.reshape(B, S, -1), ah)
    decays = jnp.stack([ret_decay_fwd, ret_decay_bwd]).astype(F32)
    v_off = kv_w // RET_HEAD_DIM
    ret_out = retention(qk_r.reshape(B, S, 2 * ret_w), vg.reshape(B, S, -1), decays, ret_norm_g.astype(F32),
                        rh, v_off, v_off + rh)
    x1 = outproj(attn_out.reshape(T, attn_w), ret_out.reshape(T, ret_w), w_out,
                 x.reshape(T, D), mod3, S, 2).reshape(B, S, D)

    h2, aff_t = router(x1, norm2_g, mod3, w_router, 3, 4)
    pos, idx, tstart, tcount = select(aff_t, cap, min(COMBINE_TM, S))
    xe = gather_rows(idx, h2, cap)
    ye = expert_down(expert_up(xe, w1, w3), w2)
    return combine(tstart, tcount, pos, aff_t, ye, x1, mod3, final_g, cap, 5)
```
